```python
import jax
import jax.numpy as jnp
from jax import lax
import numpy as np

D_MODEL = 1024
BATCH = 8
SEQ = 8192
DEPTH = 2

N_EVEN = (DEPTH + 1) // 2
N_ODD = DEPTH // 2

HEAD_DIM = 64
ATTN_HEADS = D_MODEL // (2 * HEAD_DIM)
ATTN_KV_HEADS = max(1, ATTN_HEADS // 4)
WINDOW = 128
ATTN_BLOCK = 128

SSM_D_INNER = D_MODEL // 2
SSM_HEAD_DIM = 64
SSM_HEADS = SSM_D_INNER // SSM_HEAD_DIM
SSM_GROUPS = 2
SSM_D_STATE = 128
SSM_CONV = 4
SSM_CHUNK = 128
SSM_CONV_CH = SSM_D_INNER + 2 * SSM_GROUPS * SSM_D_STATE

Q_COLS = ATTN_HEADS * HEAD_DIM
KV_COLS = ATTN_KV_HEADS * HEAD_DIM
MIX_WIDTH = Q_COLS + SSM_D_INNER
IN_COLS = Q_COLS + 2 * KV_COLS + SSM_D_INNER + SSM_CONV_CH + SSM_HEADS

CONV_CH = D_MODEL
CONV_KERNEL = 31

MOE_GROUPS = 4
MOE_EXPERTS_PER_GROUP = 8
MOE_EXPERTS = MOE_GROUPS * MOE_EXPERTS_PER_GROUP
MOE_TOP_K = 2
EXPERT_FF = D_MODEL // 2
MOE_BLOCK = 128

RMS_EPS = 1e-6
LN_EPS = 1e-5

kernel_name = 'hybrid_swa_ssd_conformer_hmoe'


def rms_norm(x, g):
    xf = x.astype(jnp.float32)
    y = xf * lax.rsqrt(jnp.mean(xf * xf, axis=-1, keepdims=True) + RMS_EPS)
    return (y * g.astype(jnp.float32)).astype(x.dtype)


def layer_norm(x, g, b):
    xf = x.astype(jnp.float32)
    mu = jnp.mean(xf, axis=-1, keepdims=True)
    var = jnp.mean(jnp.square(xf - mu), axis=-1, keepdims=True)
    y = (xf - mu) * lax.rsqrt(var + LN_EPS)
    return (y * g.astype(jnp.float32) + b.astype(jnp.float32)).astype(x.dtype)


def causal_depthwise_conv(x, w, b):
    k_width, ch = w.shape
    y = lax.conv_general_dilated(
        x, w[:, None, :].astype(x.dtype), window_strides=(1,), padding=[(k_width - 1, 0)],
        dimension_numbers=('NWC', 'WIO', 'NWC'), feature_group_count=ch)
    return y + b.astype(x.dtype)


def sliding_window_sink_attention(q, k, v, sinks):
    bsz, seq, n_heads, dh = q.shape
    n_kv = k.shape[2]
    rep = n_heads // n_kv
    blk = ATTN_BLOCK
    nb = seq // blk
    qb = q.reshape(bsz, nb, blk, n_kv, rep, dh)
    kb = k.reshape(bsz, nb, blk, n_kv, dh)
    vb = v.reshape(bsz, nb, blk, n_kv, dh)
    pad = ((0, 0), (1, 0), (0, 0), (0, 0), (0, 0))
    kk = jnp.concatenate([jnp.pad(kb, pad)[:, :-1], kb], axis=2)
    vv = jnp.concatenate([jnp.pad(vb, pad)[:, :-1], vb], axis=2)
    s = jnp.einsum('bnqkgd,bnskd->bnkgqs', qb, kk).astype(jnp.float32) * (dh ** -0.5)
    qi = jnp.arange(blk)[:, None]
    kj = jnp.arange(2 * blk)[None, :]
    rel = qi + blk - kj
    band = (rel >= 0) & (rel < WINDOW)
    real_key = (jnp.arange(nb)[:, None, None] > 0) | (kj >= blk)[None]
    mask = band[None] & real_key
    s = jnp.where(mask[None, :, None, None], s, -jnp.inf)
    sink = sinks.astype(jnp.float32).reshape(n_kv, rep)[None, None, :, :, None, None]
    m = jnp.maximum(jnp.max(s, axis=-1, keepdims=True), sink)
    p = jnp.exp(s - m)
    denom = jnp.sum(p, axis=-1, keepdims=True) + jnp.exp(sink - m)
    probs = (p / denom).astype(v.dtype)
    o = jnp.einsum('bnkgqs,bnskd->bnqkgd', probs, vv)
    return o.reshape(bsz, seq, n_heads * dh)


def ssd_chunked_scan(x, dt, a_coef, bm, cm):
    bsz, seq, n_heads, hp = x.shape
    n_groups, n_state = bm.shape[2], bm.shape[3]
    r = n_heads // n_groups
    lc = SSM_CHUNK
    nc = seq // lc
    xc = (x.astype(jnp.float32) * dt[..., None]).reshape(bsz, nc, lc, n_groups, r, hp)
    ac = (dt * a_coef).reshape(bsz, nc, lc, n_groups, r)
    bc = bm.astype(jnp.float32).reshape(bsz, nc, lc, n_groups, n_state)
    cc = cm.astype(jnp.float32).reshape(bsz, nc, lc, n_groups, n_state)
    a_cum = jnp.cumsum(ac, axis=2)
    seg = a_cum[:, :, :, None] - a_cum[:, :, None, :]
    tril = jnp.tril(jnp.ones((lc, lc), dtype=bool))[None, None, :, :, None, None]
    decay = jnp.exp(jnp.where(tril, seg, -jnp.inf))
    cb = jnp.einsum('bclgn,bcsgn->bclsg', cc, bc)
    y_diag = jnp.einsum('bclsgr,bcsgrp->bclgrp', cb[..., None] * decay, xc)
    decay_to_end = jnp.exp(a_cum[:, :, -1:] - a_cum)
    states = jnp.einsum('bclgn,bclgrp->bcgrpn', bc, decay_to_end[..., None] * xc)
    chunk_decay = jnp.exp(a_cum[:, :, -1])

    def step(h, inp):
        st, dec = inp
        return h * dec[..., None, None] + st, h

    h0 = jnp.zeros((bsz, n_groups, r, hp, n_state), jnp.float32)
    _, h_in = lax.scan(step, h0, (jnp.swapaxes(states, 0, 1), jnp.swapaxes(chunk_decay, 0, 1)))
    h_in = jnp.swapaxes(h_in, 0, 1)
    y_off = jnp.einsum('bclgn,bcgrpn->bclgrp', cc, h_in) * jnp.exp(a_cum)[..., None]
    return (y_diag + y_off).reshape(bsz, seq, n_heads, hp)


def mamba2_heads(z, xbc, dt, conv_w, conv_b, dt_bias, a_log, d_skip, out_norm_g):
    bsz, seq, _ = z.shape
    xbc = jax.nn.silu(causal_depthwise_conv(xbc, conv_w, conv_b))
    xs, bm, cm = jnp.split(xbc, [SSM_D_INNER, SSM_D_INNER + SSM_GROUPS * SSM_D_STATE], axis=-1)
    dt = jax.nn.softplus(dt.astype(jnp.float32) + dt_bias.astype(jnp.float32))
    a_coef = -jnp.exp(a_log.astype(jnp.float32))
    xh = xs.reshape(bsz, seq, SSM_HEADS, SSM_HEAD_DIM)
    y = ssd_chunked_scan(xh, dt, a_coef,
                         bm.reshape(bsz, seq, SSM_GROUPS, SSM_D_STATE),
                         cm.reshape(bsz, seq, SSM_GROUPS, SSM_D_STATE))
    y = y + d_skip.astype(jnp.float32)[:, None] * xh.astype(jnp.float32)
    y = y.reshape(bsz, seq, SSM_D_INNER) * jax.nn.silu(z.astype(jnp.float32))
    return rms_norm(y, out_norm_g).astype(z.dtype)


def attn_ssd_mixer(h, w_in, q_norm_g, k_norm_g, sinks, conv_w, conv_b, dt_bias, a_log, d_skip,
                   out_norm_g, w_out):
    bsz, seq, _ = h.shape
    proj = h @ w_in
    cuts = np.cumsum([Q_COLS, KV_COLS, KV_COLS, SSM_D_INNER, SSM_CONV_CH]).tolist()
    q, k, v, z, xbc, dt = jnp.split(proj, cuts, axis=-1)
    q = rms_norm(q.reshape(bsz, seq, ATTN_HEADS, HEAD_DIM), q_norm_g)
    k = rms_norm(k.reshape(bsz, seq, ATTN_KV_HEADS, HEAD_DIM), k_norm_g)
    v = v.reshape(bsz, seq, ATTN_KV_HEADS, HEAD_DIM)
    y_attn = sliding_window_sink_attention(q, k, v, sinks)
    y_ssm = mamba2_heads(z, xbc, dt, conv_w, conv_b, dt_bias, a_log, d_skip, out_norm_g)
    return jnp.concatenate([y_attn, y_ssm], axis=-1) @ w_out


def conformer_conv_module(h, w_pw1, b_pw1, w_dw, b_dw, ln_g, ln_b, w_pw2, b_pw2):
    u = h @ w_pw1 + b_pw1
    u = u[..., :CONV_CH] * jax.nn.sigmoid(u[..., CONV_CH:])
    u = causal_depthwise_conv(u, w_dw, b_dw)
    u = jax.nn.silu(layer_norm(u, ln_g, ln_b))
    return u @ w_pw2 + b_pw2


def hierarchical_moe(h, w_group, b_group, w_expert, b_expert, w_gate_up, w_down):
    bsz, seq, dm = h.shape
    n_tok = bsz * seq
    xt = h.reshape(n_tok, dm)
    g_logits = (xt @ w_group).astype(jnp.float32) + b_group.astype(jnp.float32)
    g_sel = jnp.argmax(g_logits, axis=-1)
    g_w = jnp.take_along_axis(jax.nn.softmax(g_logits, axis=-1), g_sel[:, None], axis=-1)[:, 0]
    e_all = (xt @ w_expert).astype(jnp.float32).reshape(n_tok, MOE_GROUPS, MOE_EXPERTS_PER_GROUP)
    e_all = e_all + b_expert.astype(jnp.float32)[None]
    e_logits = jnp.take_along_axis(e_all, g_sel[:, None, None], axis=1)[:, 0]
    top_v, top_i = lax.top_k(e_logits, MOE_TOP_K)
    gate = (g_w[:, None] * jax.nn.softmax(top_v, axis=-1)).reshape(-1)
    eid = (g_sel[:, None] * MOE_EXPERTS_PER_GROUP + top_i).reshape(-1)
    n_slots = n_tok * MOE_TOP_K
    tok = jnp.arange(n_slots) // MOE_TOP_K
    order = jnp.argsort(eid)
    eid_s, tok_s, gate_s = eid[order], tok[order], gate[order]
    counts = jnp.bincount(eid, length=MOE_EXPERTS)
    starts = jnp.cumsum(counts) - counts
    padded = (counts + MOE_BLOCK - 1) // MOE_BLOCK * MOE_BLOCK
    pends = jnp.cumsum(padded)
    pstarts = pends - padded
    row = pstarts[eid_s] + (jnp.arange(n_slots) - starts[eid_s])
    n_rows = n_slots + MOE_EXPERTS * MOE_BLOCK
    n_blocks = n_rows // MOE_BLOCK
    buf = jnp.zeros((n_rows, dm), h.dtype).at[row].set(xt[tok_s])
    block_expert = jnp.minimum(
        jnp.searchsorted(pends, jnp.arange(n_blocks) * MOE_BLOCK, side='right'), MOE_EXPERTS - 1)

    def expert_block(args):
        xb, e = args
        gu = xb @ w_gate_up[e]
        return (jax.nn.silu(gu[:, :EXPERT_FF]) * gu[:, EXPERT_FF:]) @ w_down[e]

    out = lax.map(expert_block, (buf.reshape(n_blocks, MOE_BLOCK, dm), block_expert))
    y_s = out.reshape(n_rows, dm)[row] * gate_s[:, None].astype(h.dtype)
    y = jnp.zeros((n_tok, dm), h.dtype).at[tok_s].add(y_s)
    return y.reshape(bsz, seq, dm)


def setup_inputs(seed: int = 0) -> dict:
    key = jax.random.key(seed)
    ks = iter(jax.random.split(key, 48))

    def nrm(shape, scale):
        return jax.random.normal(next(ks), shape, jnp.float32) * scale

    def gain(shape):
        return 1.0 + nrm(shape, 0.05)

    ne, no = N_EVEN, N_ODD
    u = jax.random.uniform(next(ks), (ne, SSM_HEADS), jnp.float32)
    dt0 = jnp.exp(u * (np.log(0.1) - np.log(0.001)) + np.log(0.001)).astype(jnp.float32)
    dt_bias = dt0 + jnp.log(-jnp.expm1(-dt0))
    a_log = jnp.log(jax.random.uniform(next(ks), (ne, SSM_HEADS), jnp.float32, 1.0, 16.0))
    return {
        'x': nrm((BATCH, SEQ, D_MODEL), 1.0),
        'mix_norm_g': gain((ne, D_MODEL)),
        'w_in': nrm((ne, D_MODEL, IN_COLS), D_MODEL ** -0.5),
        'q_norm_g': gain((ne, HEAD_DIM)),
        'k_norm_g': gain((ne, HEAD_DIM)),
        'attn_sinks': nrm((ne, ATTN_HEADS), 1.0),
        'ssm_conv_w': nrm((ne, SSM_CONV, SSM_CONV_CH), SSM_CONV ** -0.5),
        'ssm_conv_b': nrm((ne, SSM_CONV_CH), 0.02),
        'ssm_dt_bias': dt_bias,
        'ssm_a_log': a_log,
        'ssm_d': gain((ne, SSM_HEADS)),
        'ssm_out_norm_g': gain((ne, SSM_D_INNER)),
        'w_out': nrm((ne, MIX_WIDTH, D_MODEL), MIX_WIDTH ** -0.5),
        'conv_norm_g': gain((no, D_MODEL)),
        'conv_w_pw1': nrm((no, D_MODEL, 2 * CONV_CH), D_MODEL ** -0.5),
        'conv_b_pw1': nrm((no, 2 * CONV_CH), 0.02),
        'conv_w_dw': nrm((no, CONV_KERNEL, CONV_CH), CONV_KERNEL ** -0.5),
        'conv_b_dw': nrm((no, CONV_CH), 0.02),
        'conv_ln_g': gain((no, CONV_CH)),
        'conv_ln_b': nrm((no, CONV_CH), 0.02),
        'conv_w_pw2': nrm((no, CONV_CH, D_MODEL), CONV_CH ** -0.5),
        'conv_b_pw2': nrm((no, D_MODEL), 0.02),
        'moe_norm_g': gain((DEPTH, D_MODEL)),
        'moe_w_group': nrm((DEPTH, D_MODEL, MOE_GROUPS), D_MODEL ** -0.5),
        'moe_b_group': nrm((DEPTH, MOE_GROUPS), 0.01),
        'moe_w_expert': nrm((DEPTH, D_MODEL, MOE_EXPERTS), D_MODEL ** -0.5),
        'moe_b_expert': nrm((DEPTH, MOE_GROUPS, MOE_EXPERTS_PER_GROUP), 0.01),
        'moe_w_gate_up': nrm((DEPTH, MOE_EXPERTS, D_MODEL, 2 * EXPERT_FF), D_MODEL ** -0.5),
        'moe_w_down': nrm((DEPTH, MOE_EXPERTS, EXPERT_FF, D_MODEL), EXPERT_FF ** -0.5),
    }


def reference(x, mix_norm_g, w_in, q_norm_g, k_norm_g, attn_sinks, ssm_conv_w, ssm_conv_b,
              ssm_dt_bias, ssm_a_log, ssm_d, ssm_out_norm_g, w_out,
              conv_norm_g, conv_w_pw1, conv_b_pw1, conv_w_dw, conv_b_dw, conv_ln_g, conv_ln_b,
              conv_w_pw2, conv_b_pw2,
              moe_norm_g, moe_w_group, moe_b_group, moe_w_expert, moe_b_expert,
              moe_w_gate_up, moe_w_down):
    for layer in range(DEPTH):
        i = layer // 2
        if layer % 2 == 0:
            x = x + attn_ssd_mixer(rms_norm(x, mix_norm_g[i]), w_in[i], q_norm_g[i], k_norm_g[i],
                                   attn_sinks[i], ssm_conv_w[i], ssm_conv_b[i], ssm_dt_bias[i],
                                   ssm_a_log[i], ssm_d[i], ssm_out_norm_g[i], w_out[i])
        else:
            x = x + conformer_conv_module(rms_norm(x, conv_norm_g[i]), conv_w_pw1[i], conv_b_pw1[i],
                                          conv_w_dw[i], conv_b_dw[i], conv_ln_g[i], conv_ln_b[i],
                                          conv_w_pw2[i], conv_b_pw2[i])
        x = x + hierarchical_moe(rms_norm(x, moe_norm_g[layer]), moe_w_group[layer],
                                 moe_b_group[layer], moe_w_expert[layer], moe_b_expert[layer],
                                 moe_w_gate_up[layer], moe_w_down[layer])
    return x
```

```python
import functools

import numpy as np
import jax
import jax.numpy as jnp
from jax import lax
from jax.experimental import pallas as pl
from jax.experimental.pallas import tpu as pltpu

F32 = jnp.float32
BF16 = jnp.bfloat16
I32 = jnp.int32
U32 = jnp.uint32

HEAD_DIM = 64
GQA_REP = 4
ATTN_BLOCK = 128
SSM_HEAD_DIM = 64
SSM_GROUPS = 2
SSM_D_STATE = 128
SSM_CHUNK = 128
MOE_TOP_K = 2
RMS_EPS = 1e-6
LN_EPS = 1e-5

LANES = 128
SUBLANES = 8
MXU_DIM = 256

ROW_TILE = 512
EXPERT_ROWS = 256
CONV_TILE = 256
CONV_HALO = 32
CONV_ROWS = 16
ROUTE_ROWS = 8
VMEM_LIMIT = 48 * 1024 * 1024


def _cparams(*sem):
    return pltpu.CompilerParams(dimension_semantics=sem, vmem_limit_bytes=VMEM_LIMIT)


def _dot(a, b):
    return jnp.dot(a, b, preferred_element_type=F32)


def _dot_exact(a, b):
    return jnp.dot(a, b, preferred_element_type=F32, precision=lax.Precision.HIGHEST)


def _rms(x, g):
    return x * lax.rsqrt(jnp.mean(x * x, axis=-1, keepdims=True) + RMS_EPS) * g


def _silu(x):
    return x * jax.nn.sigmoid(x)


def _pack_halves(y):
    c = y.shape[1] // 2
    hi = lax.bitcast_convert_type(y[:, :c].astype(BF16).astype(F32), U32)
    lo = lax.bitcast_convert_type(y[:, c:].astype(BF16).astype(F32), U32)
    return (hi & jnp.uint32(0xFFFF0000)) | (lo >> 16)


def _unpack_halves(u):
    hi = lax.bitcast_convert_type(u & jnp.uint32(0xFFFF0000), F32)
    lo = lax.bitcast_convert_type(u << 16, F32)
    return jnp.concatenate([hi, lo], axis=1)


def _inproj_kernel(x_ref, g_ref, wq_ref, wk_ref, wv_ref, wz_ref, wx_ref, wdt_ref, qg_ref, kg_ref, bd_ref,
                   q_out, k_out, v_out, z_out, xbc_out, dt_out):
    xn = _rms(x_ref[...], g_ref[...]).astype(BF16)
    bd = bd_ref[...]

    def head_rms(y, gain):
        sq = y * y
        hi = sq.astype(BF16)
        lo = (sq - hi.astype(F32)).astype(BF16)
        ss = _dot(hi, bd) + _dot(lo, bd)
        return y * lax.rsqrt(ss * (1.0 / HEAD_DIM) + RMS_EPS) * gain

    q = head_rms(_dot(xn, wq_ref[...]), qg_ref[...])
    q_out[...] = (q * (HEAD_DIM ** -0.5)).astype(BF16)
    k_out[...] = head_rms(_dot(xn, wk_ref[...]), kg_ref[...]).astype(BF16)
    v_out[...] = _dot(xn, wv_ref[...]).astype(BF16)
    z_out[...] = _dot(xn, wz_ref[...]).astype(BF16)
    xbc_out[...] = _dot(xn, wx_ref[...]).astype(BF16)
    dt_out[...] = _dot(xn, wdt_ref[...])


def _inproj(x2, g, wq, wk, wv, wz, wx, wdt, qg, kg, bd):
    t, d = x2.shape
    tm = ROW_TILE
    row = lambda n: pl.BlockSpec((tm, n), lambda i: (i, 0))
    full = lambda a: pl.BlockSpec(a.shape, lambda i: (0,) * a.ndim)
    outs = [(wq.shape[1], BF16), (wk.shape[1], BF16), (wv.shape[1], BF16), (wz.shape[1], BF16),
            (wx.shape[1], BF16), (wdt.shape[1], F32)]
    return pl.pallas_call(
        _inproj_kernel,
        grid=(t // tm,),
        in_specs=[row(d)] + [full(a) for a in (g, wq, wk, wv, wz, wx, wdt, qg, kg, bd)],
        out_specs=[row(n) for n, _ in outs],
        out_shape=[jax.ShapeDtypeStruct((t, n), dt) for n, dt in outs],
        compiler_params=_cparams("parallel"),
        name="inproj",
    )(x2, g, wq, wk, wv, wz, wx, wdt, qg, kg, bd)


def _attn_kernel(sink_ref, q_ref, kc_ref, kp_ref, vc_ref, vp_ref, o_ref):
    n = pl.program_id(1)
    blk = q_ref.shape[0]
    gw = GQA_REP * HEAD_DIM
    n_kv = q_ref.shape[1] // gw
    qi = lax.broadcasted_iota(I32, (blk, 2 * blk), 0)
    kj = lax.broadcasted_iota(I32, (blk, 2 * blk), 1)
    rel = qi + blk - kj
    first_key = jnp.where(n > 0, 0, blk)
    band = (rel >= 0) & (rel < ATTN_BLOCK) & (kj >= first_key)
    key_head = lax.broadcasted_iota(I32, (2 * blk, gw), 1) // HEAD_DIM
    out_head = lax.broadcasted_iota(I32, (blk, gw), 1) // HEAD_DIM
    for g in range(n_kv):
        sl = slice(g * gw, (g + 1) * gw)
        qg = q_ref[:, sl]
        kk = jnp.concatenate([kp_ref[:, sl], kc_ref[:, sl]], axis=0)
        vv = jnp.concatenate([vp_ref[:, sl], vc_ref[:, sl]], axis=0)
        zero = jnp.zeros_like(kk)
        kbd = jnp.concatenate([jnp.where(key_head == h, kk, zero) for h in range(GQA_REP)], axis=0)
        vbd = jnp.concatenate([jnp.where(key_head == h, vv, zero) for h in range(GQA_REP)], axis=0)
        s = lax.dot_general(qg, kbd, (((1,), (1,)), ((), ())), preferred_element_type=F32)
        probs, scale = [], None
        for h in range(GQA_REP):
            sh = jnp.where(band, s[:, h * 2 * blk:(h + 1) * 2 * blk], -jnp.inf)
            sink = sink_ref[0, g * GQA_REP + h]
            m = jnp.maximum(jnp.max(sh, axis=-1, keepdims=True), sink)
            p = jnp.exp(sh - m)
            denom = jnp.sum(p, axis=-1, keepdims=True) + jnp.exp(sink - m)
            probs.append(p.astype(BF16))
            inv = jnp.broadcast_to(1.0 / denom, (blk, gw))
            scale = inv if scale is None else jnp.where(out_head == h, inv, scale)
        o = _dot(jnp.concatenate(probs, axis=1), vbd)
        o_ref[:, sl] = (o * scale).astype(BF16)


def _attention(sinks, q, k, v, bsz, seq):
    blk = ATTN_BLOCK
    nb = seq // blk
    w = q.shape[1]
    cur = pl.BlockSpec((blk, w), lambda b, n: (b * nb + n, 0))
    prev = pl.BlockSpec((blk, w), lambda b, n: (b * nb + jnp.maximum(n - 1, 0), 0))
    return pl.pallas_call(
        _attn_kernel,
        grid=(bsz, nb),
        in_specs=[pl.BlockSpec(memory_space=pltpu.SMEM), cur, cur, prev, cur, prev],
        out_specs=cur,
        out_shape=jax.ShapeDtypeStruct(q.shape, BF16),
        compiler_params=_cparams("parallel", "parallel"),
        name="swa_attention",
    )(sinks, q, k, k, v, v)


def _ssd_kernel(xbc_ref, z_ref, dt_ref, cw_ref, cb_ref, dtb_ref, alog_ref, dskip_ref, og_ref, expand_ref, tri_ref,
                y_ref, tail_ref, state_ref):
    c = pl.program_id(1)
    lc = xbc_ref.shape[0]
    d_inner = z_ref.shape[1]
    n_heads = d_inner // SSM_HEAD_DIM
    gn = SSM_GROUPS * SSM_D_STATE
    hpg = n_heads // SSM_GROUPS
    gw = hpg * SSM_HEAD_DIM
    n_tap = cw_ref.shape[0]

    @pl.when(c == 0)
    def _():
        tail_ref[...] = jnp.zeros_like(tail_ref)
        state_ref[...] = jnp.zeros_like(state_ref)

    xf = xbc_ref[...].astype(F32)
    xcat = jnp.concatenate([tail_ref[...], xf], axis=0)
    acc = jnp.broadcast_to(cb_ref[...], xf.shape)
    for k in range(n_tap):
        off = SUBLANES - (n_tap - 1) + k
        acc = acc + cw_ref[k:k + 1, :] * xcat[off:off + lc, :]
    tail_ref[...] = xf[lc - SUBLANES:, :]
    u = _silu(acc)
    xs = u[:, :d_inner]
    bm = u[:, d_inner:d_inner + gn].astype(BF16)
    cm = u[:, d_inner + gn:].astype(BF16)

    expand = expand_ref[...]
    dtr = dt_ref[...] + dtb_ref[...]
    dt_c = jnp.maximum(dtr, 0.0) + jnp.log1p(jnp.exp(-jnp.abs(dtr)))
    a_c = dt_c * (-jnp.exp(alog_ref[...]))
    acum_c = _dot_exact(tri_ref[...], a_c)
    acum_r = acum_c.T
    dt_e = _dot_exact(dt_c, expand)
    acum_e = _dot_exact(acum_c, expand)
    xdt = xs * dt_e

    li = lax.broadcasted_iota(I32, (lc, lc), 0)
    si = lax.broadcasted_iota(I32, (lc, lc), 1)
    causal = li >= si
    mats = []
    for g in range(SSM_GROUPS):
        cb = lax.dot_general(cm[:, g * SSM_D_STATE:(g + 1) * SSM_D_STATE], bm[:, g * SSM_D_STATE:(g + 1) * SSM_D_STATE],
                             (((1,), (1,)), ((), ())), preferred_element_type=F32)
        for r in range(hpg):
            h = g * hpg + r
            seg = acum_c[:, h:h + 1] - acum_r[h:h + 1, :]
            decay = jnp.exp(jnp.where(causal, seg, -jnp.inf))
            mats.append((cb * decay).astype(BF16))
    xdt_b = xdt.astype(BF16)
    row_head = lax.broadcasted_iota(I32, (lc, d_inner), 1) // SSM_HEAD_DIM
    zero = jnp.zeros_like(xdt_b)
    xbd = jnp.concatenate([jnp.where(row_head == h, xdt_b, zero) for h in range(n_heads)], axis=0)
    y = _dot(jnp.concatenate(mats, axis=1), xbd)

    a_last = acum_e[lc - 1:lc, :]
    w_state = (xdt * jnp.exp(a_last - acum_e)).astype(BF16)
    y_off = []
    for g in range(SSM_GROUPS):
        ns = slice(g * SSM_D_STATE, (g + 1) * SSM_D_STATE)
        hs = slice(g * gw, (g + 1) * gw)
        h_in = state_ref[:, hs]
        y_off.append(_dot(cm[:, ns], h_in.astype(BF16)))
        new = lax.dot_general(bm[:, ns], w_state[:, hs], (((0,), (0,)), ((), ())), preferred_element_type=F32)
        state_ref[:, hs] = h_in * jnp.exp(a_last[:, hs]) + new
    y = y + jnp.concatenate(y_off, axis=1) * jnp.exp(acum_e)

    y = y + dskip_ref[...] * xs
    y = y * _silu(z_ref[...].astype(F32))
    y_ref[...] = _rms(y, og_ref[...]).astype(BF16)


def _ssd(xbc, z, dt, cw, cb, dtb, alog, dskip, og, expand, tri, bsz, seq):
    lc = SSM_CHUNK
    nc = seq // lc
    blk = lambda n: pl.BlockSpec((lc, n), lambda b, c: (b * nc + c, 0))
    full = lambda a: pl.BlockSpec(a.shape, lambda b, c: (0,) * a.ndim)
    d_inner = z.shape[1]
    return pl.pallas_call(
        _ssd_kernel,
        grid=(bsz, nc),
        in_specs=[blk(xbc.shape[1]), blk(d_inner), blk(dt.shape[1])]
        + [full(a) for a in (cw, cb, dtb, alog, dskip, og, expand, tri)],
        out_specs=blk(d_inner),
        out_shape=jax.ShapeDtypeStruct(z.shape, BF16),
        scratch_shapes=[pltpu.VMEM((SUBLANES, xbc.shape[1]), F32), pltpu.VMEM((SSM_D_STATE, d_inner), F32)],
        compiler_params=_cparams("arbitrary", "arbitrary"),
        name="ssd_scan",
    )(xbc, z, dt, cw, cb, dtb, alog, dskip, og, expand, tri)


def _proj_router_kernel(n_act, n_groups, n_experts, *refs):
    x_ref = refs[0]
    act_refs = refs[1:1 + n_act]
    w_refs = refs[1 + n_act:1 + 2 * n_act]
    b_ref, g_ref, wr_ref, br_ref, tri_ref = refs[1 + 2 * n_act:6 + 2 * n_act]
    xnew_ref, xnp_ref, gates_ref, route_ref, counts_ref, run_ref = refs[6 + 2 * n_act:]
    i = pl.program_id(0)
    tm = x_ref.shape[0]
    epg = n_experts // n_groups

    @pl.when(i == 0)
    def _():
        run_ref[...] = jnp.zeros_like(run_ref)

    y = x_ref[...] + b_ref[...]
    for a_ref, w_ref in zip(act_refs, w_refs):
        y = y + _dot(a_ref[...], w_ref[...])
    xnew_ref[...] = y
    xn = _rms(y, g_ref[...])
    xnp_ref[...] = _pack_halves(xn)

    logits = _dot(xn.astype(BF16), wr_ref[...]) + br_ref[...]
    lane = lax.broadcasted_iota(I32, logits.shape, 1)
    big = jnp.int32(LANES)
    neg = -jnp.inf

    def first_argmax(v):
        m = jnp.max(v, axis=-1, keepdims=True)
        return m, jnp.min(jnp.where(v == m, lane, big), axis=-1, keepdims=True)

    gl = jnp.where(lane < n_groups, logits, neg)
    gmax, gsel = first_argmax(gl)
    g_w = 1.0 / jnp.sum(jnp.exp(gl - gmax), axis=-1, keepdims=True)
    elane = lane - n_groups
    in_group = (elane >= 0) & (elane < n_experts) & ((elane // epg) == gsel)
    el = jnp.where(in_group, logits, neg)
    top1, i1 = first_argmax(el)
    top2, i2 = first_argmax(jnp.where(lane == i1, neg, el))
    e2 = jnp.exp(top2 - top1)
    w1 = 1.0 / (1.0 + e2)
    gate1 = g_w * w1
    gate2 = g_w * (e2 * w1)
    gates_ref[...] = jnp.where(lane == 0, gate1, jnp.where(lane == 1, gate2, 0.0))

    sel1 = lane == i1
    sel2 = lane == i2
    onehot = jnp.where(sel1 | sel2, 1.0, 0.0)
    before = _dot(tri_ref[...], onehot.astype(BF16)) + run_ref[...]
    rank1 = jnp.sum(jnp.where(sel1, before, 0.0), axis=-1, keepdims=True)
    rank2 = jnp.sum(jnp.where(sel2, before, 0.0), axis=-1, keepdims=True)
    run_ref[...] = run_ref[...] + jnp.sum(onehot, axis=0, keepdims=True)
    counts_ref[...] = run_ref[...]

    table = jnp.where(lane == 0, (i1 - n_groups).astype(F32),
                      jnp.where(lane == 1, (i2 - n_groups).astype(F32),
                                jnp.where(lane == 2, rank1, jnp.where(lane == 3, rank2, 0.0))))
    route_ref[...] = table.T[:ROUTE_ROWS, :].astype(I32)


def _proj_router(x2, acts, ws, bias, g, wr, br, tri, n_groups, n_experts):
    t, d = x2.shape
    tm = ROW_TILE
    row = lambda n: pl.BlockSpec((tm, n), lambda i: (i, 0))
    full = lambda a: pl.BlockSpec(a.shape, lambda i: (0,) * a.ndim)
    kern = functools.partial(_proj_router_kernel, len(acts), n_groups, n_experts)
    return pl.pallas_call(
        kern,
        grid=(t // tm,),
        in_specs=[row(d)] + [row(a.shape[1]) for a in acts] + [full(a) for a in (*ws, bias, g, wr, br, tri)],
        out_specs=[row(d), row(d // 2), row(LANES), pl.BlockSpec((ROUTE_ROWS, tm), lambda i: (0, i)),
                   pl.BlockSpec((1, LANES), lambda i: (0, 0))],
        out_shape=[jax.ShapeDtypeStruct((t, d), F32), jax.ShapeDtypeStruct((t, d // 2), U32),
                   jax.ShapeDtypeStruct((t, LANES), F32), jax.ShapeDtypeStruct((ROUTE_ROWS, t), I32),
                   jax.ShapeDtypeStruct((1, LANES), F32)],
        scratch_shapes=[pltpu.VMEM((1, LANES), F32)],
        compiler_params=_cparams("arbitrary"),
        name="proj_router",
    )(x2, *acts, *ws, bias, g, wr, br, tri)


def _slot_row(pstart_ref, route_ref, t, k):
    return pstart_ref[route_ref[k, t]] + route_ref[MOE_TOP_K + k, t]


def _dispatch_kernel(pstart_ref, route_ref, x_ref, buf_in_ref, buf_ref, sem):
    del buf_in_ref
    tm = x_ref.shape[0]

    def issue(t, carry):
        for k in range(MOE_TOP_K):
            row = _slot_row(pstart_ref, route_ref, t, k)
            pltpu.make_async_copy(x_ref.at[pl.ds(t, 1)], buf_ref.at[pl.ds(row, 1)], sem).start()
        return carry

    lax.fori_loop(0, tm, issue, 0, unroll=8)
    for _ in range(MOE_TOP_K):
        pltpu.make_async_copy(x_ref, buf_ref.at[pl.ds(0, tm)], sem).wait()


def _dispatch(pstart, route, xnp, buf):
    t, w = xnp.shape
    tm = ROW_TILE
    grid_spec = pltpu.PrefetchScalarGridSpec(
        num_scalar_prefetch=1,
        grid=(t // tm,),
        in_specs=[pl.BlockSpec((ROUTE_ROWS, tm), lambda i, p: (0, i), memory_space=pltpu.SMEM),
                  pl.BlockSpec((tm, w), lambda i, p: (i, 0)),
                  pl.BlockSpec(memory_space=pl.ANY)],
        out_specs=pl.BlockSpec(memory_space=pl.ANY),
        scratch_shapes=[pltpu.SemaphoreType.DMA(())],
    )
    return pl.pallas_call(
        _dispatch_kernel,
        grid_spec=grid_spec,
        out_shape=jax.ShapeDtypeStruct(buf.shape, buf.dtype),
        input_output_aliases={3: 0},
        compiler_params=_cparams("arbitrary"),
        name="moe_dispatch",
    )(pstart, route, xnp, buf)


def _experts_kernel(be_ref, nused_ref, x_ref, wgu_ref, wd_ref, y_ref):
    del be_ref
    live = pl.program_id(0) < nused_ref[0]

    @pl.when(jnp.logical_not(live))
    def _():
        y_ref[...] = jnp.zeros_like(y_ref)

    @pl.when(live)
    def _():
        ff = wd_ref.shape[0]
        x = _unpack_halves(x_ref[...]).astype(BF16)
        gu = _dot(x, wgu_ref[...])
        h = (_silu(gu[:, :ff]) * gu[:, ff:]).astype(BF16)
        y_ref[...] = _pack_halves(_dot(h, wd_ref[...]))


def _experts(block_expert, n_used, buf, wgu, wd):
    n_rows, w = buf.shape
    d, ff2 = wgu.shape[1:]
    rows = EXPERT_ROWS
    live = lambda i, be, nu: jnp.minimum(i, nu[0] - 1)
    grid_spec = pltpu.PrefetchScalarGridSpec(
        num_scalar_prefetch=2,
        grid=(n_rows // rows,),
        in_specs=[pl.BlockSpec((rows, w), lambda i, be, nu: (live(i, be, nu), 0)),
                  pl.BlockSpec((None, d, ff2), lambda i, be, nu: (be[i], 0, 0)),
                  pl.BlockSpec((None, ff2 // 2, d), lambda i, be, nu: (be[i], 0, 0))],
        out_specs=pl.BlockSpec((rows, w), lambda i, be, nu: (i, 0)),
    )
    return pl.pallas_call(
        _experts_kernel,
        grid_spec=grid_spec,
        out_shape=jax.ShapeDtypeStruct((n_rows, w), U32),
        compiler_params=_cparams("arbitrary"),
        name="moe_experts",
    )(block_expert, n_used, buf, wgu, wd)


def _combine_kernel(pstart_ref, route_ref, x_ref, gates_ref, y_ref, o_ref, ybuf, sem):
    tm = x_ref.shape[0]

    def issue(t, carry):
        for k in range(MOE_TOP_K):
            row = _slot_row(pstart_ref, route_ref, t, k)
            pltpu.make_async_copy(y_ref.at[pl.ds(row, 1)], ybuf.at[k, pl.ds(t, 1)], sem).start()
        return carry

    lax.fori_loop(0, tm, issue, 0, unroll=8)
    for k in range(MOE_TOP_K):
        pltpu.make_async_copy(y_ref.at[pl.ds(0, tm)], ybuf.at[k], sem).wait()
    gates = gates_ref[...]
    moe = gates[:, 0:1] * _unpack_halves(ybuf[0]) + gates[:, 1:2] * _unpack_halves(ybuf[1])
    o_ref[...] = x_ref[...] + moe


def _combine(pstart, route, xnew, gates, y):
    t, d = xnew.shape
    tm = ROW_TILE
    grid_spec = pltpu.PrefetchScalarGridSpec(
        num_scalar_prefetch=1,
        grid=(t // tm,),
        in_specs=[pl.BlockSpec((ROUTE_ROWS, tm), lambda i, p: (0, i), memory_space=pltpu.SMEM),
                  pl.BlockSpec((tm, d), lambda i, p: (i, 0)),
                  pl.BlockSpec((tm, LANES), lambda i, p: (i, 0)),
                  pl.BlockSpec(memory_space=pl.ANY)],
        out_specs=pl.BlockSpec((tm, d), lambda i, p: (i, 0)),
        scratch_shapes=[pltpu.VMEM((MOE_TOP_K, tm, y.shape[1]), U32), pltpu.SemaphoreType.DMA(())],
    )
    return pl.pallas_call(
        _combine_kernel,
        grid_spec=grid_spec,
        out_shape=jax.ShapeDtypeStruct((t, d), F32),
        compiler_params=_cparams("arbitrary"),
        name="moe_combine",
    )(pstart, route, xnew, gates, y)


def _moe(xnew, xnp, gates, route, counts, wgu, wd, n_groups):
    t = xnew.shape[0]
    n_experts = wgu.shape[0]
    rows = EXPERT_ROWS
    n_rows = t * MOE_TOP_K + n_experts * rows
    n_blocks = n_rows // rows
    cnt = counts[0, n_groups:n_groups + n_experts].astype(I32)
    padded = (cnt + rows - 1) // rows * rows
    pends = jnp.cumsum(padded)
    pstart = (pends - padded).astype(I32)
    n_used = (pends[-1] // rows).astype(I32)
    blk = jnp.arange(n_blocks, dtype=I32)
    first_row = jnp.minimum(blk, n_used - 1) * rows
    be = jnp.minimum(jnp.sum(pends[None, :] <= first_row[:, None], axis=1), n_experts - 1)
    buf = _dispatch(pstart, route, xnp, jnp.zeros((n_rows, xnp.shape[1]), U32))
    y = _experts(be.astype(I32), n_used.reshape(1), buf, wgu, wd)
    return _combine(pstart, route, xnew, gates, y)


def _pw1_kernel(x_ref, g_ref, w_ref, b_ref, u_ref):
    ch = u_ref.shape[1]
    xn = _rms(x_ref[...], g_ref[...]).astype(BF16)
    y = _dot(xn, w_ref[...]) + b_ref[...]
    u_ref[...] = (y[:, :ch] * jax.nn.sigmoid(y[:, ch:])).astype(BF16)


def _pw1(x2, g, w, b):
    t, d = x2.shape
    tm = ROW_TILE
    ch = w.shape[1] // 2
    full = lambda a: pl.BlockSpec(a.shape, lambda i: (0,) * a.ndim)
    return pl.pallas_call(
        _pw1_kernel,
        grid=(t // tm,),
        in_specs=[pl.BlockSpec((tm, d), lambda i: (i, 0)), full(g), full(w), full(b)],
        out_specs=pl.BlockSpec((tm, ch), lambda i: (i, 0)),
        out_shape=jax.ShapeDtypeStruct((t, ch), BF16),
        compiler_params=_cparams("parallel"),
        name="pw1_glu",
    )(x2, g, w, b)


def _dwconv_kernel(cur_ref, halo_ref, w_ref, b_ref, lg_ref, lb_ref, o_ref, xs_ref):
    ts = cur_ref.shape[0]
    n_tap = w_ref.shape[0]
    halo = halo_ref[...].astype(F32)
    xs_ref[:CONV_HALO, :] = jnp.where(pl.program_id(1) > 0, halo, jnp.zeros_like(halo))
    xs_ref[CONV_HALO:, :] = cur_ref[...].astype(F32)
    for r in range(ts // CONV_ROWS):
        base = r * CONV_ROWS + CONV_HALO - (n_tap - 1)
        acc = jnp.broadcast_to(b_ref[...], (CONV_ROWS, o_ref.shape[1]))
        for k in range(n_tap):
            acc = acc + w_ref[k:k + 1, :] * xs_ref[base + k:base + k + CONV_ROWS, :]
        mu = jnp.mean(acc, axis=-1, keepdims=True)
        cen = acc - mu
        var = jnp.mean(cen * cen, axis=-1, keepdims=True)
        v = cen * lax.rsqrt(var + LN_EPS) * lg_ref[...] + lb_ref[...]
        o_ref[r * CONV_ROWS:(r + 1) * CONV_ROWS, :] = _silu(v).astype(BF16)


def _dwconv(u, w, b, lg, lb, bsz, seq):
    ts = CONV_TILE
    nt = seq // ts
    ch = u.shape[1]
    hpt = ts // CONV_HALO
    full = lambda a: pl.BlockSpec(a.shape, lambda bb, i: (0,) * a.ndim)
    return pl.pallas_call(
        _dwconv_kernel,
        grid=(bsz, nt),
        in_specs=[pl.BlockSpec((ts, ch), lambda bb, i: (bb * nt + i, 0)),
                  pl.BlockSpec((CONV_HALO, ch), lambda bb, i: (jnp.maximum((bb * nt + i) * hpt - 1, 0), 0)),
                  full(w), full(b), full(lg), full(lb)],
        out_specs=pl.BlockSpec((ts, ch), lambda bb, i: (bb * nt + i, 0)),
        out_shape=jax.ShapeDtypeStruct(u.shape, BF16),
        scratch_shapes=[pltpu.VMEM((CONV_HALO + ts, ch), F32)],
        compiler_params=_cparams("parallel", "parallel"),
        name="dwconv_ln",
    )(u, u, w, b, lg, lb)


def _row(v):
    return v.reshape(1, -1).astype(F32)


def _pad_lanes(v, n=LANES):
    v = v.reshape(1, -1).astype(F32)
    return jnp.pad(v, ((0, 0), (0, n - v.shape[1])))


def _router_params(w_group, b_group, w_expert, b_expert):
    wr = jnp.concatenate([w_group, w_expert], axis=1)
    wr = jnp.pad(wr, ((0, 0), (0, LANES - wr.shape[1]))).astype(BF16)
    br = _pad_lanes(jnp.concatenate([b_group.reshape(-1), b_expert.reshape(-1)]))
    return wr, br


def kernel(x, mix_norm_g, w_in, q_norm_g, k_norm_g, attn_sinks, ssm_conv_w, ssm_conv_b, ssm_dt_bias, ssm_a_log, ssm_d, ssm_out_norm_g, w_out, conv_norm_g, conv_w_pw1, conv_b_pw1, conv_w_dw, conv_b_dw, conv_ln_g, conv_ln_b, conv_w_pw2, conv_b_pw2, moe_norm_g, moe_w_group, moe_b_group, moe_w_expert, moe_b_expert, moe_w_gate_up, moe_w_down):
    bsz, seq, d = x.shape
    t = bsz * seq
    n_heads = attn_sinks.shape[1]
    n_kv = n_heads // GQA_REP
    q_cols = n_heads * HEAD_DIM
    kv_cols = n_kv * HEAD_DIM
    d_inner = ssm_out_norm_g.shape[1]
    ssm_heads = ssm_a_log.shape[1]
    n_groups = moe_w_group.shape[2]
    n_experts = moe_w_expert.shape[2]
    assert t % ROW_TILE == 0 and seq % CONV_TILE == 0 and seq % SSM_CHUNK == 0 and seq % ATTN_BLOCK == 0
    assert ssm_heads <= LANES and n_groups + n_experts <= LANES

    x2 = x.reshape(t, d)
    tri_strict = jnp.asarray(np.tril(np.ones((ROW_TILE, ROW_TILE), np.float32), -1), BF16)
    zero_bias = jnp.zeros((1, d), F32)

    w = w_in[0]
    cuts = [0] + np.cumsum([q_cols, kv_cols, kv_cols, d_inner, ssm_conv_w.shape[2]]).tolist() + [w.shape[1]]
    wq, wk, wv, wz, wx, wdt = (w[:, lo:hi] for lo, hi in zip(cuts[:-1], cuts[1:]))
    rep = lambda m: jnp.repeat(m.reshape(d, n_kv, 1, HEAD_DIM), GQA_REP, axis=2).reshape(d, n_kv * GQA_REP * HEAD_DIM)
    wdt = jnp.pad(wdt, ((0, 0), (0, LANES - ssm_heads)))
    seg_ones = jnp.asarray(np.kron(np.eye(q_cols // HEAD_DIM, dtype=np.float32),
                                   np.ones((HEAD_DIM, HEAD_DIM), np.float32)), BF16)
    qg = jnp.tile(_row(q_norm_g[0]), (1, n_heads))
    kg = jnp.tile(_row(k_norm_g[0]), (1, n_heads))
    q, k, v, z, xbc, dt = _inproj(
        x2, _row(mix_norm_g[0]), wq.astype(BF16), rep(wk).astype(BF16), rep(wv).astype(BF16), wz.astype(BF16),
        wx.astype(BF16), wdt.astype(BF16), qg, kg, seg_ones)
    y_attn = _attention(attn_sinks[0].reshape(1, n_heads).astype(F32), q, k, v, bsz, seq)
    expand = jnp.asarray(np.kron(np.eye(LANES, ssm_heads, dtype=np.float32),
                                 np.ones((1, SSM_HEAD_DIM), np.float32)), F32)
    tri_incl = jnp.asarray(np.tril(np.ones((SSM_CHUNK, SSM_CHUNK), np.float32)), F32)
    y_ssm = _ssd(xbc, z, dt, ssm_conv_w[0].astype(F32), _row(ssm_conv_b[0]), _pad_lanes(ssm_dt_bias[0]),
                 _pad_lanes(ssm_a_log[0]), jnp.repeat(_row(ssm_d[0]), SSM_HEAD_DIM, axis=1), _row(ssm_out_norm_g[0]),
                 expand, tri_incl, bsz, seq)
    wo = w_out[0].astype(BF16)
    wr, br = _router_params(moe_w_group[0], moe_b_group[0], moe_w_expert[0], moe_b_expert[0])
    xnew, xnp, gates, route, counts = _proj_router(
        x2, [y_attn, y_ssm], [wo[:q_cols], wo[q_cols:]], zero_bias, _row(moe_norm_g[0]), wr, br, tri_strict,
        n_groups, n_experts)
    x2 = _moe(xnew, xnp, gates, route, counts, moe_w_gate_up[0].astype(BF16), moe_w_down[0].astype(BF16), n_groups)

    u = _pw1(x2, _row(conv_norm_g[0]), conv_w_pw1[0].astype(BF16), _row(conv_b_pw1[0]))
    u = _dwconv(u, conv_w_dw[0].astype(F32), _row(conv_b_dw[0]), _row(conv_ln_g[0]), _row(conv_ln_b[0]), bsz, seq)
    wr, br = _router_params(moe_w_group[1], moe_b_group[1], moe_w_expert[1], moe_b_expert[1])
    xnew, xnp, gates, route, counts = _proj_router(
        x2, [u], [conv_w_pw2[0].astype(BF16)], _row(conv_b_pw2[0]), _row(moe_norm_g[1]), wr, br, tri_strict,
        n_groups, n_experts)
    x2 = _moe(xnew, xnp, gates, route, counts, moe_w_gate_up[1].astype(BF16), moe_w_down[1].astype(BF16), n_groups)
    return x2.reshape(bsz, seq, d)
```

```python
import functools

import numpy as np
import jax
import jax.numpy as jnp
from jax import lax
from jax.experimental import pallas as pl
from jax.experimental.pallas import tpu as pltpu

F32 = jnp.float32
BF16 = jnp.bfloat16
I32 = jnp.int32
U32 = jnp.uint32

HEAD_DIM = 64
GQA_REP = 4
ATTN_BLOCK = 128
SSM_HEAD_DIM = 64
SSM_GROUPS = 2
SSM_D_STATE = 128
SSM_CHUNK = 128
MOE_TOP_K = 2
RMS_EPS = 1e-6
LN_EPS = 1e-5

LANES = 128
SUBLANES = 8

ROW_TILE = 512
EXPERT_ROWS = 256
CONV_TILE = 256
CONV_HALO = 32
CONV_ROWS = 64
RUN_ROWS = SUBLANES
VMEM_LIMIT = 48 * 1024 * 1024


def _cparams(*sem):
    return pltpu.CompilerParams(dimension_semantics=sem, vmem_limit_bytes=VMEM_LIMIT)


def _dot(a, b):
    return jnp.dot(a, b, preferred_element_type=F32)


def _dot_exact(a, b):
    return jnp.dot(a, b, preferred_element_type=F32, precision=lax.Precision.HIGHEST)


def _rms(x, g):
    return x * lax.rsqrt(jnp.mean(x * x, axis=-1, keepdims=True) + RMS_EPS) * g


def _silu(x):
    return x * jax.nn.sigmoid(x)


def _pack_halves(y):
    c = y.shape[1] // 2
    hi = lax.bitcast_convert_type(y[:, :c].astype(BF16).astype(F32), U32)
    lo = lax.bitcast_convert_type(y[:, c:].astype(BF16).astype(F32), U32)
    return (hi & jnp.uint32(0xFFFF0000)) | (lo >> 16)


def _unpack_halves(u):
    hi = lax.bitcast_convert_type(u & jnp.uint32(0xFFFF0000), F32)
    lo = lax.bitcast_convert_type(u << 16, F32)
    return jnp.concatenate([hi, lo], axis=1)


def _inproj_kernel(x_ref, g_ref, wq_ref, wk_ref, wv_ref, wz_ref, wx_ref, wdt_ref, qg_ref, kg_ref, bd_ref,
                   q_out, k_out, v_out, z_out, xbc_out, dt_out):
    xn = _rms(x_ref[...], g_ref[...]).astype(BF16)
    bd = bd_ref[...]

    def head_rms(y, gain):
        sq = y * y
        hi = sq.astype(BF16)
        lo = (sq - hi.astype(F32)).astype(BF16)
        ss = _dot(hi, bd) + _dot(lo, bd)
        return y * lax.rsqrt(ss * (1.0 / HEAD_DIM) + RMS_EPS) * gain

    q = head_rms(_dot(xn, wq_ref[...]), qg_ref[...])
    q_out[...] = (q * (HEAD_DIM ** -0.5)).astype(BF16)
    k_out[...] = head_rms(_dot(xn, wk_ref[...]), kg_ref[...]).astype(BF16)
    v_out[...] = _dot(xn, wv_ref[...]).astype(BF16)
    z_out[...] = _dot(xn, wz_ref[...]).astype(BF16)
    xbc_out[...] = _dot(xn, wx_ref[...]).astype(BF16)
    dt_out[...] = _dot(xn, wdt_ref[...])


def _inproj(x2, g, wq, wk, wv, wz, wx, wdt, qg, kg, bd):
    t, d = x2.shape
    tm = ROW_TILE
    row = lambda n: pl.BlockSpec((tm, n), lambda i: (i, 0))
    full = lambda a: pl.BlockSpec(a.shape, lambda i: (0,) * a.ndim)
    outs = [(wq.shape[1], BF16), (wk.shape[1], BF16), (wv.shape[1], BF16), (wz.shape[1], BF16),
            (wx.shape[1], BF16), (wdt.shape[1], F32)]
    return pl.pallas_call(
        _inproj_kernel,
        grid=(t // tm,),
        in_specs=[row(d)] + [full(a) for a in (g, wq, wk, wv, wz, wx, wdt, qg, kg, bd)],
        out_specs=[row(n) for n, _ in outs],
        out_shape=[jax.ShapeDtypeStruct((t, n), dt) for n, dt in outs],
        compiler_params=_cparams("parallel"),
        name="inproj",
    )(x2, g, wq, wk, wv, wz, wx, wdt, qg, kg, bd)


def _attn_kernel(sink_ref, q_ref, kc_ref, kp_ref, vc_ref, vp_ref, o_ref):
    n = pl.program_id(1)
    blk = q_ref.shape[0]
    gw = GQA_REP * HEAD_DIM
    n_kv = q_ref.shape[1] // gw
    qi = lax.broadcasted_iota(I32, (blk, 2 * blk), 0)
    kj = lax.broadcasted_iota(I32, (blk, 2 * blk), 1)
    rel = qi + blk - kj
    first_key = jnp.where(n > 0, 0, blk)
    band = (rel >= 0) & (rel < ATTN_BLOCK) & (kj >= first_key)
    key_head = lax.broadcasted_iota(I32, (2 * blk, gw), 1) // HEAD_DIM
    out_head = lax.broadcasted_iota(I32, (blk, gw), 1) // HEAD_DIM
    for g in range(n_kv):
        sl = slice(g * gw, (g + 1) * gw)
        qg = q_ref[:, sl]
        kk = jnp.concatenate([kp_ref[:, sl], kc_ref[:, sl]], axis=0)
        vv = jnp.concatenate([vp_ref[:, sl], vc_ref[:, sl]], axis=0)
        zero = jnp.zeros_like(kk)
        kbd = jnp.concatenate([jnp.where(key_head == h, kk, zero) for h in range(GQA_REP)], axis=0)
        vbd = jnp.concatenate([jnp.where(key_head == h, vv, zero) for h in range(GQA_REP)], axis=0)
        s = lax.dot_general(qg, kbd, (((1,), (1,)), ((), ())), preferred_element_type=F32)
        probs, scale = [], None
        for h in range(GQA_REP):
            sh = jnp.where(band, s[:, h * 2 * blk:(h + 1) * 2 * blk], -jnp.inf)
            sink = sink_ref[0, g * GQA_REP + h]
            m = jnp.maximum(jnp.max(sh, axis=-1, keepdims=True), sink)
            p = jnp.exp(sh - m)
            denom = jnp.sum(p, axis=-1, keepdims=True) + jnp.exp(sink - m)
            probs.append(p.astype(BF16))
            inv = jnp.broadcast_to(1.0 / denom, (blk, gw))
            scale = inv if scale is None else jnp.where(out_head == h, inv, scale)
        o = _dot(jnp.concatenate(probs, axis=1), vbd)
        o_ref[:, sl] = (o * scale).astype(BF16)


def _attention(sinks, q, k, v, bsz, seq):
    blk = ATTN_BLOCK
    nb = seq // blk
    w = q.shape[1]
    cur = pl.BlockSpec((blk, w), lambda b, n: (b * nb + n, 0))
    prev = pl.BlockSpec((blk, w), lambda b, n: (b * nb + jnp.maximum(n - 1, 0), 0))
    return pl.pallas_call(
        _attn_kernel,
        grid=(bsz, nb),
        in_specs=[pl.BlockSpec(memory_space=pltpu.SMEM), cur, cur, prev, cur, prev],
        out_specs=cur,
        out_shape=jax.ShapeDtypeStruct(q.shape, BF16),
        compiler_params=_cparams("parallel", "parallel"),
        name="swa_attention",
    )(sinks, q, k, k, v, v)


def _ssd_kernel(xbc_ref, z_ref, dt_ref, cw_ref, cb_ref, dtb_ref, alog_ref, dskip_ref, og_ref, expand_ref, tri_ref,
                y_ref, tail_ref, state_ref):
    c = pl.program_id(1)
    lc = xbc_ref.shape[0]
    d_inner = z_ref.shape[1]
    n_heads = d_inner // SSM_HEAD_DIM
    gn = SSM_GROUPS * SSM_D_STATE
    hpg = n_heads // SSM_GROUPS
    gw = hpg * SSM_HEAD_DIM
    n_tap = cw_ref.shape[0]

    @pl.when(c == 0)
    def _():
        tail_ref[...] = jnp.zeros_like(tail_ref)
        state_ref[...] = jnp.zeros_like(state_ref)

    xf = xbc_ref[...].astype(F32)
    xcat = jnp.concatenate([tail_ref[...], xf], axis=0)
    acc = jnp.broadcast_to(cb_ref[...], xf.shape)
    for k in range(n_tap):
        off = SUBLANES - (n_tap - 1) + k
        acc = acc + cw_ref[k:k + 1, :] * xcat[off:off + lc, :]
    tail_ref[...] = xf[lc - SUBLANES:, :]
    u = _silu(acc)
    xs = u[:, :d_inner]
    bm = u[:, d_inner:d_inner + gn].astype(BF16)
    cm = u[:, d_inner + gn:].astype(BF16)

    expand = expand_ref[...]
    dtr = dt_ref[...] + dtb_ref[...]
    dt_c = jnp.maximum(dtr, 0.0) + jnp.log1p(jnp.exp(-jnp.abs(dtr)))
    a_c = dt_c * (-jnp.exp(alog_ref[...]))
    acum_c = _dot_exact(tri_ref[...], a_c)
    acum_r = acum_c.T
    dt_e = _dot_exact(dt_c, expand)
    acum_e = _dot_exact(acum_c, expand)
    xdt = xs * dt_e

    li = lax.broadcasted_iota(I32, (lc, lc), 0)
    si = lax.broadcasted_iota(I32, (lc, lc), 1)
    causal = li >= si
    mats = []
    for g in range(SSM_GROUPS):
        cb = lax.dot_general(cm[:, g * SSM_D_STATE:(g + 1) * SSM_D_STATE], bm[:, g * SSM_D_STATE:(g + 1) * SSM_D_STATE],
                             (((1,), (1,)), ((), ())), preferred_element_type=F32)
        for r in range(hpg):
            h = g * hpg + r
            seg = acum_c[:, h:h + 1] - acum_r[h:h + 1, :]
            decay = jnp.exp(jnp.where(causal, seg, -jnp.inf))
            mats.append((cb * decay).astype(BF16))
    xdt_b = xdt.astype(BF16)
    row_head = lax.broadcasted_iota(I32, (lc, d_inner), 1) // SSM_HEAD_DIM
    zero = jnp.zeros_like(xdt_b)
    xbd = jnp.concatenate([jnp.where(row_head == h, xdt_b, zero) for h in range(n_heads)], axis=0)
    y = _dot(jnp.concatenate(mats, axis=1), xbd)

    a_last = acum_e[lc - 1:lc, :]
    w_state = (xdt * jnp.exp(a_last - acum_e)).astype(BF16)
    y_off = []
    for g in range(SSM_GROUPS):
        ns = slice(g * SSM_D_STATE, (g + 1) * SSM_D_STATE)
        hs = slice(g * gw, (g + 1) * gw)
        h_in = state_ref[:, hs]
        y_off.append(_dot(cm[:, ns], h_in.astype(BF16)))
        new = lax.dot_general(bm[:, ns], w_state[:, hs], (((0,), (0,)), ((), ())), preferred_element_type=F32)
        state_ref[:, hs] = h_in * jnp.exp(a_last[:, hs]) + new
    y = y + jnp.concatenate(y_off, axis=1) * jnp.exp(acum_e)

    y = y + dskip_ref[...] * xs
    y = y * _silu(z_ref[...].astype(F32))
    y_ref[...] = _rms(y, og_ref[...]).astype(BF16)


def _ssd(xbc, z, dt, cw, cb, dtb, alog, dskip, og, expand, tri, bsz, seq):
    lc = SSM_CHUNK
    nc = seq // lc
    blk = lambda n: pl.BlockSpec((lc, n), lambda b, c: (b * nc + c, 0))
    full = lambda a: pl.BlockSpec(a.shape, lambda b, c: (0,) * a.ndim)
    d_inner = z.shape[1]
    return pl.pallas_call(
        _ssd_kernel,
        grid=(bsz, nc),
        in_specs=[blk(xbc.shape[1]), blk(d_inner), blk(dt.shape[1])]
        + [full(a) for a in (cw, cb, dtb, alog, dskip, og, expand, tri)],
        out_specs=blk(d_inner),
        out_shape=jax.ShapeDtypeStruct(z.shape, BF16),
        scratch_shapes=[pltpu.VMEM((SUBLANES, xbc.shape[1]), F32), pltpu.VMEM((SSM_D_STATE, d_inner), F32)],
        compiler_params=_cparams("arbitrary", "arbitrary"),
        name="ssd_scan",
    )(xbc, z, dt, cw, cb, dtb, alog, dskip, og, expand, tri)


def _sorted_rows(tm, n_experts):
    return MOE_TOP_K * tm + n_experts * RUN_ROWS


def _proj_router_kernel(n_act, n_groups, n_experts, *refs):
    x_ref = refs[0]
    act_refs = refs[1:1 + n_act]
    w_refs = refs[1 + n_act:1 + 2 * n_act]
    b_ref, g_ref, wr_ref, br_ref, tri_ref, ltri_ref = refs[1 + 2 * n_act:7 + 2 * n_act]
    xnew_ref, xs_ref, gates_ref, counts_ref = refs[7 + 2 * n_act:]
    tm = x_ref.shape[0]
    epg = n_experts // n_groups

    y = x_ref[...] + b_ref[...]
    for a_ref, w_ref in zip(act_refs, w_refs):
        y = y + _dot(a_ref[...], w_ref[...])
    xnew_ref[...] = y
    xn = _rms(y, g_ref[...]).astype(BF16)

    logits = _dot(xn, wr_ref[...]) + br_ref[...]
    lane = lax.broadcasted_iota(I32, logits.shape, 1)
    big = jnp.int32(LANES)
    neg = -jnp.inf

    def first_argmax(v):
        m = jnp.max(v, axis=-1, keepdims=True)
        return m, jnp.min(jnp.where(v == m, lane, big), axis=-1, keepdims=True)

    gl = jnp.where(lane < n_groups, logits, neg)
    gmax, gsel = first_argmax(gl)
    g_w = 1.0 / jnp.sum(jnp.exp(gl - gmax), axis=-1, keepdims=True)
    elane = lane - n_groups
    in_group = (elane >= 0) & (elane < n_experts) & ((elane // epg) == gsel)
    el = jnp.where(in_group, logits, neg)
    top1, i1 = first_argmax(el)
    top2, i2 = first_argmax(jnp.where(lane == i1, neg, el))
    e2 = jnp.exp(top2 - top1)
    w1 = 1.0 / (1.0 + e2)
    gate1 = g_w * w1
    gate2 = g_w * (e2 * w1)

    sel1 = lane == i1
    sel2 = lane == i2
    onehot = jnp.where(sel1, 1.0, jnp.where(sel2, 1.0, 0.0))
    before = _dot(tri_ref[...], onehot.astype(BF16))
    cnt = jnp.sum(onehot, axis=0, keepdims=True)
    counts_ref[...] = cnt
    cnt_pad = jnp.floor((cnt + (RUN_ROWS - 1)) * (1.0 / RUN_ROWS)) * RUN_ROWS
    run_start = _dot(jnp.broadcast_to(cnt_pad, (SUBLANES, LANES)).astype(BF16), ltri_ref[...])[0:1, :]
    slot = run_start + before
    j1 = jnp.sum(jnp.where(sel1, slot, 0.0), axis=-1, keepdims=True)
    j2 = jnp.sum(jnp.where(sel2, slot, 0.0), axis=-1, keepdims=True)
    table = jnp.where(lane == 0, gate1, jnp.where(lane == 1, gate2,
                                                  jnp.where(lane == 2, j1, jnp.where(lane == 3, j2, 0.0))))
    gates_ref[...] = table

    tt = table.T
    j1r = tt[2:3, :].astype(I32)
    j2r = tt[3:4, :].astype(I32)
    ri = lax.broadcasted_iota(I32, (xs_ref.shape[0], tm), 0)
    perm = jnp.where(ri == j1r, 1.0, jnp.where(ri == j2r, 1.0, 0.0)).astype(BF16)
    xs_ref[...] = _pack_halves(_dot(perm, xn))


def _proj_router(x2, acts, ws, bias, g, wr, br, tri, ltri, n_groups, n_experts):
    t, d = x2.shape
    tm = ROW_TILE
    n_tiles = t // tm
    lr = _sorted_rows(tm, n_experts)
    row = lambda n: pl.BlockSpec((tm, n), lambda i: (i, 0))
    full = lambda a: pl.BlockSpec(a.shape, lambda i: (0,) * a.ndim)
    kern = functools.partial(_proj_router_kernel, len(acts), n_groups, n_experts)
    return pl.pallas_call(
        kern,
        grid=(n_tiles,),
        in_specs=[row(d)] + [row(a.shape[1]) for a in acts] + [full(a) for a in (*ws, bias, g, wr, br, tri, ltri)],
        out_specs=[row(d), pl.BlockSpec((lr, d // 2), lambda i: (i, 0)), row(LANES),
                   pl.BlockSpec((None, 1, LANES), lambda i: (i, 0, 0))],
        out_shape=[jax.ShapeDtypeStruct((t, d), F32), jax.ShapeDtypeStruct((n_tiles * lr, d // 2), U32),
                   jax.ShapeDtypeStruct((t, LANES), F32), jax.ShapeDtypeStruct((n_tiles, 1, LANES), F32)],
        compiler_params=_cparams("parallel"),
        name="proj_router",
    )(x2, *acts, *ws, bias, g, wr, br, tri, ltri)


def _group_copies(table_ref, base, n, src_ref, dst_ref, sem):
    def issue(q, carry):
        g = table_ref[base + q]
        pltpu.make_async_copy(src_ref.at[pl.ds(pl.multiple_of(g * RUN_ROWS, RUN_ROWS), RUN_ROWS)],
                              dst_ref.at[pl.ds(pl.multiple_of(q * RUN_ROWS, RUN_ROWS), RUN_ROWS)], sem).start()
        return carry
    lax.fori_loop(0, n, issue, 0, unroll=8)


def _experts_kernel(be_ref, nused_ref, src_ref, xs_ref, wgu_ref, wd_ref, y_ref, xbuf, sems, wgu_b, wd_b):
    i = pl.program_id(0)
    n_used = nused_ref[0]
    rows = y_ref.shape[0]
    gpb = rows // RUN_ROWS
    slot = i % 2

    def fetch(blk, s):
        _group_copies(src_ref, blk * gpb, gpb, xs_ref, xbuf.at[s], sems.at[s])

    @pl.when(i == 0)
    def _():
        fetch(0, 0)

    @pl.when(i + 1 < n_used)
    def _():
        fetch(i + 1, 1 - slot)

    @pl.when((i == 0) | (be_ref[i] != be_ref[jnp.maximum(i - 1, 0)]))
    def _():
        wgu_b[...] = wgu_ref[...].astype(BF16)
        wd_b[...] = wd_ref[...].astype(BF16)

    @pl.when(i >= n_used)
    def _():
        y_ref[...] = jnp.zeros_like(y_ref)

    @pl.when(i < n_used)
    def _():
        pltpu.make_async_copy(xs_ref.at[pl.ds(0, rows)], xbuf.at[slot], sems.at[slot]).wait()
        ff = wd_b.shape[0]
        x = _unpack_halves(xbuf[slot]).astype(BF16)
        gu = _dot(x, wgu_b[...])
        h = (_silu(gu[:, :ff]) * gu[:, ff:]).astype(BF16)
        y_ref[...] = _pack_halves(_dot(h, wd_b[...]))


def _experts(block_expert, n_used, src_groups, xs, wgu, wd, n_blocks):
    w = xs.shape[1]
    d, ff2 = wgu.shape[1:]
    rows = EXPERT_ROWS
    grid_spec = pltpu.PrefetchScalarGridSpec(
        num_scalar_prefetch=3,
        grid=(n_blocks,),
        in_specs=[pl.BlockSpec(memory_space=pl.ANY),
                  pl.BlockSpec((None, d, ff2), lambda i, be, nu, sg: (be[i], 0, 0)),
                  pl.BlockSpec((None, ff2 // 2, d), lambda i, be, nu, sg: (be[i], 0, 0))],
        out_specs=pl.BlockSpec((rows, w), lambda i, be, nu, sg: (i, 0)),
        scratch_shapes=[pltpu.VMEM((2, rows, w), U32), pltpu.SemaphoreType.DMA((2,)),
                        pltpu.VMEM((d, ff2), BF16), pltpu.VMEM((ff2 // 2, d), BF16)],
    )
    return pl.pallas_call(
        _experts_kernel,
        grid_spec=grid_spec,
        out_shape=jax.ShapeDtypeStruct((n_blocks * rows, w), U32),
        compiler_params=_cparams("arbitrary"),
        name="moe_experts",
    )(block_expert, n_used, src_groups, xs, wgu, wd)


def _combine_kernel(dst_ref, x_ref, gates_ref, y_ref, o_ref, ybuf, sems):
    i = pl.program_id(0)
    n_tiles = pl.num_programs(0)
    tm = x_ref.shape[0]
    lr = ybuf.shape[1]
    gpt = lr // RUN_ROWS
    slot = i % 2

    def fetch(tile, s):
        _group_copies(dst_ref, tile * gpt, gpt, y_ref, ybuf.at[s], sems.at[s])

    @pl.when(i == 0)
    def _():
        fetch(0, 0)

    @pl.when(i + 1 < n_tiles)
    def _():
        fetch(i + 1, 1 - slot)

    pltpu.make_async_copy(y_ref.at[pl.ds(0, lr)], ybuf.at[slot], sems.at[slot]).wait()
    ys = _unpack_halves(ybuf[slot]).astype(BF16)
    table = gates_ref[...]
    col = lax.broadcasted_iota(I32, (tm, lr), 1)
    moe = None
    for k in range(MOE_TOP_K):
        jk = table[:, MOE_TOP_K + k:MOE_TOP_K + k + 1].astype(I32)
        pick = jnp.where(col == jk, 1.0, 0.0).astype(BF16)
        term = table[:, k:k + 1] * _dot(pick, ys)
        moe = term if moe is None else moe + term
    o_ref[...] = x_ref[...] + moe


def _combine(dst_groups, xnew, gates, y, lr):
    t, d = xnew.shape
    tm = ROW_TILE
    grid_spec = pltpu.PrefetchScalarGridSpec(
        num_scalar_prefetch=1,
        grid=(t // tm,),
        in_specs=[pl.BlockSpec((tm, d), lambda i, p: (i, 0)),
                  pl.BlockSpec((tm, LANES), lambda i, p: (i, 0)),
                  pl.BlockSpec(memory_space=pl.ANY)],
        out_specs=pl.BlockSpec((tm, d), lambda i, p: (i, 0)),
        scratch_shapes=[pltpu.VMEM((2, lr, y.shape[1]), U32), pltpu.SemaphoreType.DMA((2,))],
    )
    return pl.pallas_call(
        _combine_kernel,
        grid_spec=grid_spec,
        out_shape=jax.ShapeDtypeStruct((t, d), F32),
        compiler_params=_cparams("arbitrary"),
        name="moe_combine",
    )(dst_groups, xnew, gates, y)


def _moe_plan(counts, n_groups, n_experts, lr):
    n_tiles = counts.shape[0]
    gpb = EXPERT_ROWS // RUN_ROWS
    gpt = lr // RUN_ROWS
    cnt = counts[:, 0, n_groups:n_groups + n_experts].astype(I32)
    run = (cnt + RUN_ROWS - 1) // RUN_ROWS
    run_end = jnp.cumsum(run, axis=1)
    run_off = run_end - run
    used = run_end[:, -1]
    seg = jnp.sum(run, axis=0)
    seg_pad = (seg + gpb - 1) // gpb * gpb
    seg_end = jnp.cumsum(seg_pad)
    seg_start = seg_end - seg_pad
    before = jnp.cumsum(run, axis=0) - run
    n_blocks = (n_tiles * gpt + n_experts * (gpb - 1)) // gpb + 1
    n_used = (seg_end[-1] // gpb).astype(I32)
    blk = jnp.arange(n_blocks, dtype=I32)
    first = jnp.minimum(blk, n_used - 1) * gpb
    block_expert = jnp.minimum(jnp.sum(seg_end[None, :] <= first[:, None], axis=1), n_experts - 1).astype(I32)

    zero_dst = n_blocks * gpb - 1
    q = jnp.arange(gpt, dtype=I32)
    owner = jnp.sum(run_end[:, None, :] <= q[None, :, None], axis=2)
    own = jax.nn.one_hot(jnp.minimum(owner, n_experts - 1), n_experts, dtype=I32)
    shift = seg_start[None, :] + before - run_off
    dst = q[None, :] + jnp.sum(own * shift[:, None, :], axis=2)
    dst = jnp.where(q[None, :] < used[:, None], dst, zero_dst).astype(I32).reshape(-1)

    zero_src = gpt - 1
    g = jnp.arange(n_blocks * gpb, dtype=I32)
    ge = jnp.minimum(jnp.sum(seg_end[None, :] <= g[:, None], axis=1), n_experts - 1)
    m = g - seg_start[ge]
    tile_end = jnp.cumsum(run, axis=0).T[ge]
    tile = jnp.minimum(jnp.sum(tile_end <= m[:, None], axis=1), n_tiles - 1)
    src = tile * gpt + run_off[tile, ge] + m - before[tile, ge]
    src = jnp.where(m < seg[ge], src, zero_src).astype(I32)
    return block_expert, n_used.reshape(1), src, dst, n_blocks


def _moe(xnew, xs, gates, counts, wgu, wd, n_groups):
    n_experts = wgu.shape[0]
    lr = _sorted_rows(ROW_TILE, n_experts)
    block_expert, n_used, src, dst, n_blocks = _moe_plan(counts, n_groups, n_experts, lr)
    y = _experts(block_expert, n_used, src, xs, wgu, wd, n_blocks)
    return _combine(dst, xnew, gates, y, lr)


def _pw1_kernel(x_ref, g_ref, w_ref, b_ref, u_ref):
    ch = u_ref.shape[1]
    xn = _rms(x_ref[...], g_ref[...]).astype(BF16)
    y = _dot(xn, w_ref[...]) + b_ref[...]
    u_ref[...] = (y[:, :ch] * jax.nn.sigmoid(y[:, ch:])).astype(BF16)


def _pw1(x2, g, w, b):
    t, d = x2.shape
    tm = ROW_TILE
    ch = w.shape[1] // 2
    full = lambda a: pl.BlockSpec(a.shape, lambda i: (0,) * a.ndim)
    return pl.pallas_call(
        _pw1_kernel,
        grid=(t // tm,),
        in_specs=[pl.BlockSpec((tm, d), lambda i: (i, 0)), full(g), full(w), full(b)],
        out_specs=pl.BlockSpec((tm, ch), lambda i: (i, 0)),
        out_shape=jax.ShapeDtypeStruct((t, ch), BF16),
        compiler_params=_cparams("parallel"),
        name="pw1_glu",
    )(x2, g, w, b)


def _dwconv_kernel(cur_ref, halo_ref, w_ref, b_ref, lg_ref, lb_ref, o_ref, xs_ref, acc_ref):
    ts, ch = cur_ref.shape
    n_strip, n_tap = w_ref.shape[:2]
    n_chunk = ts // CONV_ROWS
    first_tap = CONV_HALO - (n_tap - 1)
    halo = halo_ref[...].astype(F32)
    halo = jnp.where(pl.program_id(1) > 0, halo, jnp.zeros_like(halo))
    cur = cur_ref[...].astype(F32)
    shifted_rows = ts + CONV_HALO - SUBLANES
    for c in range(n_strip):
        cs = slice(c * LANES, (c + 1) * LANES)
        xs_ref[0, c, :CONV_HALO, :] = halo[:, cs]
        xs_ref[0, c, CONV_HALO:, :] = cur[:, cs]
        for s in range(1, SUBLANES):
            xs_ref[s, c, :shifted_rows, :] = xs_ref[0, c, s:s + shifted_rows, :]

    def strip_chunk(idx, carry):
        r = idx // n_strip
        c = idx % n_strip
        row0 = pl.multiple_of(r * CONV_ROWS, CONV_ROWS)
        acc = jnp.broadcast_to(b_ref[c], (CONV_ROWS, LANES))
        for s in range(SUBLANES):
            taps = [k for k in range(n_tap) if (first_tap + k) % SUBLANES == s]
            lo = (first_tap + taps[0]) // SUBLANES * SUBLANES
            hi = (first_tap + taps[-1]) // SUBLANES * SUBLANES
            data = xs_ref[s, c, pl.ds(row0 + lo, hi - lo + CONV_ROWS), :]
            for k in taps:
                a = (first_tap + k) // SUBLANES * SUBLANES - lo
                acc = acc + w_ref[c, k:k + 1, :] * data[a:a + CONV_ROWS, :]
        acc_ref[c, pl.ds(row0, CONV_ROWS), :] = acc
        return carry

    lax.fori_loop(0, n_chunk * n_strip, strip_chunk, 0, unroll=2)
    y = jnp.concatenate([acc_ref[c] for c in range(n_strip)], axis=1)
    mu = jnp.mean(y, axis=-1, keepdims=True)
    cen = y - mu
    var = jnp.mean(cen * cen, axis=-1, keepdims=True)
    v = cen * lax.rsqrt(var + LN_EPS) * lg_ref[...] + lb_ref[...]
    o_ref[...] = _silu(v).astype(BF16)


def _dwconv(u, w, b, lg, lb, bsz, seq):
    ts = CONV_TILE
    nt = seq // ts
    ch = u.shape[1]
    hpt = ts // CONV_HALO
    n_strip = ch // LANES
    full = lambda a: pl.BlockSpec(a.shape, lambda bb, i: (0,) * a.ndim)
    w = w.reshape(-1, n_strip, LANES).transpose(1, 0, 2)
    b = b.reshape(n_strip, 1, LANES)
    return pl.pallas_call(
        _dwconv_kernel,
        grid=(bsz, nt),
        in_specs=[pl.BlockSpec((ts, ch), lambda bb, i: (bb * nt + i, 0)),
                  pl.BlockSpec((CONV_HALO, ch), lambda bb, i: (jnp.maximum((bb * nt + i) * hpt - 1, 0), 0)),
                  full(w), full(b), full(lg), full(lb)],
        out_specs=pl.BlockSpec((ts, ch), lambda bb, i: (bb * nt + i, 0)),
        out_shape=jax.ShapeDtypeStruct(u.shape, BF16),
        scratch_shapes=[pltpu.VMEM((SUBLANES, n_strip, CONV_HALO + ts, LANES), F32),
                        pltpu.VMEM((n_strip, ts, LANES), F32)],
        compiler_params=_cparams("parallel", "parallel"),
        name="dwconv_ln",
    )(u, u, w, b, lg, lb)


def _row(v):
    return v.reshape(1, -1).astype(F32)


def _pad_lanes(v, n=LANES):
    v = v.reshape(1, -1).astype(F32)
    return jnp.pad(v, ((0, 0), (0, n - v.shape[1])))


def _router_params(w_group, b_group, w_expert, b_expert):
    wr = jnp.concatenate([w_group, w_expert], axis=1)
    wr = jnp.pad(wr, ((0, 0), (0, LANES - wr.shape[1]))).astype(BF16)
    br = _pad_lanes(jnp.concatenate([b_group.reshape(-1), b_expert.reshape(-1)]))
    return wr, br


def kernel(x, mix_norm_g, w_in, q_norm_g, k_norm_g, attn_sinks, ssm_conv_w, ssm_conv_b, ssm_dt_bias, ssm_a_log, ssm_d, ssm_out_norm_g, w_out, conv_norm_g, conv_w_pw1, conv_b_pw1, conv_w_dw, conv_b_dw, conv_ln_g, conv_ln_b, conv_w_pw2, conv_b_pw2, moe_norm_g, moe_w_group, moe_b_group, moe_w_expert, moe_b_expert, moe_w_gate_up, moe_w_down):
    bsz, seq, d = x.shape
    t = bsz * seq
    n_heads = attn_sinks.shape[1]
    n_kv = n_heads // GQA_REP
    q_cols = n_heads * HEAD_DIM
    kv_cols = n_kv * HEAD_DIM
    d_inner = ssm_out_norm_g.shape[1]
    ssm_heads = ssm_a_log.shape[1]
    n_groups = moe_w_group.shape[2]
    n_experts = moe_w_expert.shape[2]
    assert t % ROW_TILE == 0 and seq % CONV_TILE == 0 and seq % SSM_CHUNK == 0 and seq % ATTN_BLOCK == 0
    assert ssm_heads <= LANES and n_groups + n_experts <= LANES

    x2 = x.reshape(t, d)
    tri_strict = jnp.asarray(np.tril(np.ones((ROW_TILE, ROW_TILE), np.float32), -1), BF16)
    lane_before = jnp.asarray(np.triu(np.ones((LANES, LANES), np.float32), 1), BF16)
    zero_bias = jnp.zeros((1, d), F32)

    w = w_in[0]
    cuts = [0] + np.cumsum([q_cols, kv_cols, kv_cols, d_inner, ssm_conv_w.shape[2]]).tolist() + [w.shape[1]]
    wq, wk, wv, wz, wx, wdt = (w[:, lo:hi] for lo, hi in zip(cuts[:-1], cuts[1:]))
    rep = lambda m: jnp.repeat(m.reshape(d, n_kv, 1, HEAD_DIM), GQA_REP, axis=2).reshape(d, n_kv * GQA_REP * HEAD_DIM)
    wdt = jnp.pad(wdt, ((0, 0), (0, LANES - ssm_heads)))
    seg_ones = jnp.asarray(np.kron(np.eye(q_cols // HEAD_DIM, dtype=np.float32),
                                   np.ones((HEAD_DIM, HEAD_DIM), np.float32)), BF16)
    qg = jnp.tile(_row(q_norm_g[0]), (1, n_heads))
    kg = jnp.tile(_row(k_norm_g[0]), (1, n_heads))
    q, k, v, z, xbc, dt = _inproj(
        x2, _row(mix_norm_g[0]), wq.astype(BF16), rep(wk).astype(BF16), rep(wv).astype(BF16), wz.astype(BF16),
        wx.astype(BF16), wdt.astype(BF16), qg, kg, seg_ones)
    y_attn = _attention(attn_sinks[0].reshape(1, n_heads).astype(F32), q, k, v, bsz, seq)
    expand = jnp.asarray(np.kron(np.eye(LANES, ssm_heads, dtype=np.float32),
                                 np.ones((1, SSM_HEAD_DIM), np.float32)), F32)
    tri_incl = jnp.asarray(np.tril(np.ones((SSM_CHUNK, SSM_CHUNK), np.float32)), F32)
    y_ssm = _ssd(xbc, z, dt, ssm_conv_w[0].astype(F32), _row(ssm_conv_b[0]), _pad_lanes(ssm_dt_bias[0]),
                 _pad_lanes(ssm_a_log[0]), jnp.repeat(_row(ssm_d[0]), SSM_HEAD_DIM, axis=1), _row(ssm_out_norm_g[0]),
                 expand, tri_incl, bsz, seq)
    wo = w_out[0].astype(BF16)
    wr, br = _router_params(moe_w_group[0], moe_b_group[0], moe_w_expert[0], moe_b_expert[0])
    xnew, xs, gates, counts = _proj_router(
        x2, [y_attn, y_ssm], [wo[:q_cols], wo[q_cols:]], zero_bias, _row(moe_norm_g[0]), wr, br, tri_strict,
        lane_before, n_groups, n_experts)
    x2 = _moe(xnew, xs, gates, counts, moe_w_gate_up[0], moe_w_down[0], n_groups)

    u = _pw1(x2, _row(conv_norm_g[0]), conv_w_pw1[0].astype(BF16), _row(conv_b_pw1[0]))
    u = _dwconv(u, conv_w_dw[0].astype(F32), _row(conv_b_dw[0]), _row(conv_ln_g[0]), _row(conv_ln_b[0]), bsz, seq)
    wr, br = _router_params(moe_w_group[1], moe_b_group[1], moe_w_expert[1], moe_b_expert[1])
    xnew, xs, gates, counts = _proj_router(
        x2, [u], [conv_w_pw2[0].astype(BF16)], _row(conv_b_pw2[0]), _row(moe_norm_g[1]), wr, br, tri_strict,
        lane_before, n_groups, n_experts)
    x2 = _moe(xnew, xs, gates, counts, moe_w_gate_up[1], moe_w_down[1], n_groups)
    return x2.reshape(bsz, seq, d)
```

```python
import functools

import numpy as np
import jax
import jax.numpy as jnp
from jax import lax
from jax.experimental import pallas as pl
from jax.experimental.pallas import tpu as pltpu

F32 = jnp.float32
BF16 = jnp.bfloat16
I32 = jnp.int32
U32 = jnp.uint32

HEAD_DIM = 64
GQA_REP = 4
ATTN_BLOCK = 128
SSM_HEAD_DIM = 64
SSM_GROUPS = 2
SSM_D_STATE = 128
SSM_CHUNK = 128
MOE_TOP_K = 2
RMS_EPS = 1e-6
LN_EPS = 1e-5

LANES = 128
SUBLANES = 8

ROW_TILE = 512
EXPERT_ROWS = 512
SSD_STEP_CHUNKS = 2
ATTN_STEP_BLOCKS = 2
CONV_TILE = 256
CONV_HALO = 32
CONV_ROWS = 64
RUN_ROWS = SUBLANES
VMEM_LIMIT = 48 * 1024 * 1024


def _cparams(*sem):
    return pltpu.CompilerParams(dimension_semantics=sem, vmem_limit_bytes=VMEM_LIMIT)


def _dot(a, b):
    return jnp.dot(a, b, preferred_element_type=F32)


def _rms(x, g):
    return x * lax.rsqrt(jnp.mean(x * x, axis=-1, keepdims=True) + RMS_EPS) * g


def _silu(x):
    return x * jax.nn.sigmoid(x)


def _pack_halves(y, is_bf16=False):
    c = y.shape[1] // 2
    if is_bf16:
        return lax.bitcast_convert_type(y[:, :c], U32) | (lax.bitcast_convert_type(y[:, c:], U32) >> 16)
    hi = lax.bitcast_convert_type(y[:, :c].astype(BF16).astype(F32), U32)
    lo = lax.bitcast_convert_type(y[:, c:].astype(BF16).astype(F32), U32)
    return (hi & jnp.uint32(0xFFFF0000)) | (lo >> 16)


def _unpack_halves(u):
    hi = lax.bitcast_convert_type(u & jnp.uint32(0xFFFF0000), F32)
    lo = lax.bitcast_convert_type(u << 16, F32)
    return jnp.concatenate([hi, lo], axis=1)


def _inproj_kernel(x_ref, g_ref, wq_ref, wk_ref, wv_ref, wz_ref, wx_ref, wdt_ref, qg_ref, kg_ref, bd_ref,
                   q_out, k_out, v_out, z_out, xbc_out, dt_out):
    xn = _rms(x_ref[...], g_ref[...]).astype(BF16)

    def head_rms(y, gain):
        bd = bd_ref[:y.shape[1], :y.shape[1]]
        sq = y * y
        hi = sq.astype(BF16)
        lo = (sq - hi.astype(F32)).astype(BF16)
        ss = _dot(hi, bd) + _dot(lo, bd)
        return y * lax.rsqrt(ss * (1.0 / HEAD_DIM) + RMS_EPS) * gain

    q = head_rms(_dot(xn, wq_ref[...]), qg_ref[...])
    q_out[...] = (q * (HEAD_DIM ** -0.5)).astype(BF16)
    k_out[...] = head_rms(_dot(xn, wk_ref[...]), kg_ref[...]).astype(BF16)
    v_out[...] = _dot(xn, wv_ref[...]).astype(BF16)
    z_out[...] = _dot(xn, wz_ref[...]).astype(BF16)
    xbc_out[...] = _dot(xn, wx_ref[...]).astype(BF16)
    dt_out[...] = _dot(xn, wdt_ref[...])


def _inproj(x2, g, wq, wk, wv, wz, wx, wdt, qg, kg, bd):
    t, d = x2.shape
    tm = ROW_TILE
    row = lambda n: pl.BlockSpec((tm, n), lambda i: (i, 0))
    full = lambda a: pl.BlockSpec(a.shape, lambda i: (0,) * a.ndim)
    outs = [(wq.shape[1], BF16), (wk.shape[1], BF16), (wv.shape[1], BF16), (wz.shape[1], BF16),
            (wx.shape[1], BF16), (wdt.shape[1], F32)]
    return pl.pallas_call(
        _inproj_kernel,
        grid=(t // tm,),
        in_specs=[row(d)] + [full(a) for a in (g, wq, wk, wv, wz, wx, wdt, qg, kg, bd)],
        out_specs=[row(n) for n, _ in outs],
        out_shape=[jax.ShapeDtypeStruct((t, n), dt) for n, dt in outs],
        compiler_params=_cparams("parallel"),
        name="inproj",
    )(x2, g, wq, wk, wv, wz, wx, wdt, qg, kg, bd)


def _attn_kernel(sink_ref, rep_ref, q_ref, kc_ref, kp_ref, vc_ref, vp_ref, o_ref):
    n = pl.program_id(1)
    blk = ATTN_BLOCK
    gw = GQA_REP * HEAD_DIM
    n_kv = q_ref.shape[1] // gw
    qi = lax.broadcasted_iota(I32, (blk, 2 * blk), 0)
    kj = lax.broadcasted_iota(I32, (blk, 2 * blk), 1)
    rel = qi + blk - kj
    in_window = (rel >= 0) & (rel < ATTN_BLOCK)
    first_key = jnp.where(n > 0, 0, blk)
    key_head = lax.broadcasted_iota(I32, (2 * blk, gw), 1) // HEAD_DIM
    out_head = lax.broadcasted_iota(I32, (blk, gw), 1) // HEAD_DIM
    for sb, g in [(sb, g) for sb in range(ATTN_STEP_BLOCKS) for g in range(n_kv)]:
        sl = slice(g * gw, (g + 1) * gw)
        rows = slice(sb * blk, (sb + 1) * blk)
        before = slice((sb - 1) * blk, sb * blk)
        band = in_window & (kj >= first_key) if sb == 0 else in_window
        qg = q_ref[rows, sl]
        kcat = jnp.concatenate([kp_ref[...] if sb == 0 else kc_ref[before, :], kc_ref[rows, :]], axis=0)
        vcat = jnp.concatenate([vp_ref[...] if sb == 0 else vc_ref[before, :], vc_ref[rows, :]], axis=0)
        kk = _dot(kcat, rep_ref[g]).astype(BF16)
        vv = _dot(vcat, rep_ref[g]).astype(BF16)
        zero = jnp.zeros_like(kk)
        kbd = jnp.concatenate([jnp.where(key_head == h, kk, zero) for h in range(GQA_REP)], axis=0)
        vbd = jnp.concatenate([jnp.where(key_head == h, vv, zero) for h in range(GQA_REP)], axis=0)
        s = lax.dot_general(qg, kbd, (((1,), (1,)), ((), ())), preferred_element_type=F32)
        probs, scale = [], None
        for h in range(GQA_REP):
            sh = jnp.where(band, s[:, h * 2 * blk:(h + 1) * 2 * blk], -jnp.inf)
            sink = sink_ref[0, g * GQA_REP + h]
            m = jnp.maximum(jnp.max(sh, axis=-1, keepdims=True), sink)
            p = jnp.exp(sh - m)
            denom = jnp.sum(p, axis=-1, keepdims=True) + jnp.exp(sink - m)
            probs.append(p.astype(BF16))
            inv = jnp.broadcast_to(1.0 / denom, (blk, gw))
            scale = inv if scale is None else jnp.where(out_head == h, inv, scale)
        o = _dot(jnp.concatenate(probs, axis=1), vbd)
        o_ref[rows, sl] = (o * scale).astype(BF16)


def _attention(sinks, kv_rep, q, k, v, bsz, seq):
    blk = ATTN_BLOCK
    spb = ATTN_STEP_BLOCKS
    nb = seq // (blk * spb)
    cur = lambda w: pl.BlockSpec((blk * spb, w), lambda b, n: (b * nb + n, 0))
    prev = lambda w: pl.BlockSpec((blk, w), lambda b, n: (jnp.maximum((b * nb + n) * spb - 1, 0), 0))
    qw, kw = q.shape[1], k.shape[1]
    return pl.pallas_call(
        _attn_kernel,
        grid=(bsz, nb),
        in_specs=[pl.BlockSpec(memory_space=pltpu.SMEM), pl.BlockSpec(kv_rep.shape, lambda b, n: (0, 0, 0)),
                  cur(qw), cur(kw), prev(kw), cur(kw), prev(kw)],
        out_specs=cur(qw),
        out_shape=jax.ShapeDtypeStruct(q.shape, BF16),
        compiler_params=_cparams("parallel", "parallel"),
        name="swa_attention",
    )(sinks, kv_rep, q, k, k, v, v)


def _split3(a):
    a1 = a.astype(BF16)
    r = a - a1.astype(F32)
    a2 = r.astype(BF16)
    return a1, a2, (r - a2.astype(F32)).astype(BF16)


def _ssd_kernel(xbc_ref, z_ref, dt_ref, cw_ref, cb_ref, dtb_ref, alog_ref, dskip_ref, og_ref, expand_ref, tri_ref,
                shift_ref, y_ref, prev_ref, state_ref):
    lc = SSM_CHUNK

    @pl.when(pl.program_id(1) == 0)
    def _():
        prev_ref[...] = jnp.zeros_like(prev_ref)
        state_ref[...] = jnp.zeros_like(state_ref)

    for cc in range(SSD_STEP_CHUNKS):
        rows = slice(cc * lc, (cc + 1) * lc)
        prev = prev_ref[...] if cc == 0 else xbc_ref[(cc - 1) * lc:cc * lc, :]
        _ssd_chunk(xbc_ref[rows, :], prev, z_ref[rows, :], dt_ref[rows, :], cw_ref, cb_ref, dtb_ref, alog_ref,
                   dskip_ref, og_ref, expand_ref, tri_ref, shift_ref, y_ref.at[rows, :], state_ref)
    prev_ref[...] = xbc_ref[(SSD_STEP_CHUNKS - 1) * lc:, :]


def _ssd_chunk(xb, prev, z, dt, cw_ref, cb_ref, dtb_ref, alog_ref, dskip_ref, og_ref, expand_ref, tri_ref, shift_ref,
               y_ref, state_ref):
    lc = xb.shape[0]
    d_inner = z.shape[1]
    n_heads = d_inner // SSM_HEAD_DIM
    gn = SSM_GROUPS * SSM_D_STATE
    hpg = n_heads // SSM_GROUPS
    gw = hpg * SSM_HEAD_DIM
    n_tap = cw_ref.shape[0]

    sh = _dot(shift_ref[...], jnp.concatenate([prev, xb], axis=0))
    acc = cb_ref[...] + cw_ref[n_tap - 1:n_tap, :] * xb.astype(F32)
    for j in range(1, n_tap):
        acc = acc + cw_ref[n_tap - 1 - j:n_tap - j, :] * sh[(j - 1) * lc:j * lc, :]
    u = _silu(acc)
    xs = u[:, :d_inner]
    bm = u[:, d_inner:d_inner + gn].astype(BF16)
    cm = u[:, d_inner + gn:].astype(BF16)

    dtr = dt + dtb_ref[...]
    dt_c = jnp.maximum(dtr, 0.0) + jnp.log1p(jnp.exp(-jnp.abs(dtr)))
    a_c = dt_c * (-jnp.exp(alog_ref[...]))
    cum = _dot(tri_ref[...], jnp.concatenate(_split3(a_c), axis=1))
    acum_c = (cum[:, :LANES] + cum[:, LANES:2 * LANES]) + cum[:, 2 * LANES:]
    acum_r = acum_c.T
    ex = _dot(jnp.concatenate(_split3(dt_c) + _split3(acum_c), axis=0), expand_ref[...])
    dt_e = (ex[:lc] + ex[lc:2 * lc]) + ex[2 * lc:3 * lc]
    acum_e = (ex[3 * lc:4 * lc] + ex[4 * lc:5 * lc]) + ex[5 * lc:]
    xdt = xs * dt_e

    li = lax.broadcasted_iota(I32, (lc, lc), 0)
    si = lax.broadcasted_iota(I32, (lc, lc), 1)
    causal = li >= si
    mats = []
    for g in range(SSM_GROUPS):
        cb = lax.dot_general(cm[:, g * SSM_D_STATE:(g + 1) * SSM_D_STATE], bm[:, g * SSM_D_STATE:(g + 1) * SSM_D_STATE],
                             (((1,), (1,)), ((), ())), preferred_element_type=F32)
        for r in range(hpg):
            h = g * hpg + r
            seg = acum_c[:, h:h + 1] - acum_r[h:h + 1, :]
            decay = jnp.exp(jnp.where(causal, seg, -jnp.inf))
            mats.append((cb * decay).astype(BF16))
    xdt_b = xdt.astype(BF16)
    row_head = lax.broadcasted_iota(I32, (lc, d_inner), 1) // SSM_HEAD_DIM
    zero = jnp.zeros_like(xdt_b)
    xbd = jnp.concatenate([jnp.where(row_head == h, xdt_b, zero) for h in range(n_heads)], axis=0)
    y = _dot(jnp.concatenate(mats, axis=1), xbd)

    a_last = acum_e[lc - 1:lc, :]
    w_state = (xdt * jnp.exp(a_last - acum_e)).astype(BF16)
    y_off = []
    for g in range(SSM_GROUPS):
        ns = slice(g * SSM_D_STATE, (g + 1) * SSM_D_STATE)
        hs = slice(g * gw, (g + 1) * gw)
        h_in = state_ref[:, hs]
        y_off.append(_dot(cm[:, ns], h_in.astype(BF16)))
        new = lax.dot_general(bm[:, ns], w_state[:, hs], (((0,), (0,)), ((), ())), preferred_element_type=F32)
        state_ref[:, hs] = h_in * jnp.exp(a_last[:, hs]) + new
    y = y + jnp.concatenate(y_off, axis=1) * jnp.exp(acum_e)

    y = y + dskip_ref[...] * xs
    y = y * _silu(z.astype(F32))
    y_ref[...] = _rms(y, og_ref[...]).astype(BF16)


def _ssd(xbc, z, dt, cw, cb, dtb, alog, dskip, og, expand, tri, shift, bsz, seq):
    rows = SSM_CHUNK * SSD_STEP_CHUNKS
    ns = seq // rows
    blk = lambda n: pl.BlockSpec((rows, n), lambda b, c: (b * ns + c, 0))
    full = lambda a: pl.BlockSpec(a.shape, lambda b, c: (0,) * a.ndim)
    d_inner = z.shape[1]
    return pl.pallas_call(
        _ssd_kernel,
        grid=(bsz, ns),
        in_specs=[blk(xbc.shape[1]), blk(d_inner), blk(dt.shape[1])]
        + [full(a) for a in (cw, cb, dtb, alog, dskip, og, expand, tri, shift)],
        out_specs=blk(d_inner),
        out_shape=jax.ShapeDtypeStruct(z.shape, BF16),
        scratch_shapes=[pltpu.VMEM((SSM_CHUNK, xbc.shape[1]), BF16), pltpu.VMEM((SSM_D_STATE, d_inner), F32)],
        compiler_params=_cparams("arbitrary", "arbitrary"),
        name="ssd_scan",
    )(xbc, z, dt, cw, cb, dtb, alog, dskip, og, expand, tri, shift)


def _sorted_rows(tm, n_experts):
    return MOE_TOP_K * tm + n_experts * RUN_ROWS


def _proj_router_kernel(n_act, n_groups, n_experts, *refs):
    x_ref = refs[0]
    act_refs = refs[1:1 + n_act]
    w_refs = refs[1 + n_act:1 + 2 * n_act]
    b_ref, g_ref, wr_ref, br_ref, tri_ref, ltri_ref = refs[1 + 2 * n_act:7 + 2 * n_act]
    xnew_ref, xs_ref, gates_ref, counts_ref = refs[7 + 2 * n_act:]
    tm = x_ref.shape[0]
    epg = n_experts // n_groups

    y = x_ref[...] + b_ref[...]
    for a_ref, w_ref in zip(act_refs, w_refs):
        y = y + _dot(a_ref[...], w_ref[...])
    xnew_ref[...] = y
    xn = _rms(y, g_ref[...]).astype(BF16)

    logits = _dot(xn, wr_ref[...]) + br_ref[...]
    lane = lax.broadcasted_iota(I32, logits.shape, 1)
    big = jnp.int32(LANES)
    neg = -jnp.inf

    def first_argmax(v):
        m = jnp.max(v, axis=-1, keepdims=True)
        return m, jnp.min(jnp.where(v == m, lane, big), axis=-1, keepdims=True)

    gl = jnp.where(lane < n_groups, logits, neg)
    gmax, gsel = first_argmax(gl)
    g_w = 1.0 / jnp.sum(jnp.exp(gl - gmax), axis=-1, keepdims=True)
    elane = lane - n_groups
    in_group = (elane >= 0) & (elane < n_experts) & ((elane // epg) == gsel)
    el = jnp.where(in_group, logits, neg)
    top1, i1 = first_argmax(el)
    top2, i2 = first_argmax(jnp.where(lane == i1, neg, el))
    e2 = jnp.exp(top2 - top1)
    w1 = 1.0 / (1.0 + e2)
    gate1 = g_w * w1
    gate2 = g_w * (e2 * w1)

    sel1 = lane == i1
    sel2 = lane == i2
    onehot = jnp.where(sel1, 1.0, jnp.where(sel2, 1.0, 0.0))
    before = _dot(tri_ref[...], onehot.astype(BF16))
    cnt = jnp.sum(onehot, axis=0, keepdims=True)
    counts_ref[...] = cnt
    cnt_pad = jnp.floor((cnt + (RUN_ROWS - 1)) * (1.0 / RUN_ROWS)) * RUN_ROWS
    run_start = _dot(jnp.broadcast_to(cnt_pad, (SUBLANES, LANES)).astype(BF16), ltri_ref[...])[0:1, :]
    slot = run_start + before
    j1 = jnp.sum(jnp.where(sel1, slot, 0.0), axis=-1, keepdims=True)
    j2 = jnp.sum(jnp.where(sel2, slot, 0.0), axis=-1, keepdims=True)
    table = jnp.where(lane == 0, gate1, jnp.where(lane == 1, gate2,
                                                  jnp.where(lane == 2, j1, jnp.where(lane == 3, j2, 0.0))))
    gates_ref[...] = table

    tt = table.T
    j1r = tt[2:3, :].astype(I32)
    j2r = tt[3:4, :].astype(I32)
    ri = lax.broadcasted_iota(I32, (xs_ref.shape[0], tm), 0)
    perm = jnp.where(ri == j1r, 1.0, jnp.where(ri == j2r, 1.0, 0.0)).astype(BF16)
    xs_ref[...] = _pack_halves(_dot(perm, xn), is_bf16=True)


def _proj_router(x2, acts, ws, bias, g, wr, br, tri, ltri, n_groups, n_experts):
    t, d = x2.shape
    tm = ROW_TILE
    n_tiles = t // tm
    lr = _sorted_rows(tm, n_experts)
    row = lambda n: pl.BlockSpec((tm, n), lambda i: (i, 0))
    full = lambda a: pl.BlockSpec(a.shape, lambda i: (0,) * a.ndim)
    kern = functools.partial(_proj_router_kernel, len(acts), n_groups, n_experts)
    return pl.pallas_call(
        kern,
        grid=(n_tiles,),
        in_specs=[row(d)] + [row(a.shape[1]) for a in acts] + [full(a) for a in (*ws, bias, g, wr, br, tri, ltri)],
        out_specs=[row(d), pl.BlockSpec((lr, d // 2), lambda i: (i, 0)), row(LANES),
                   pl.BlockSpec((None, 1, LANES), lambda i: (i, 0, 0))],
        out_shape=[jax.ShapeDtypeStruct((t, d), F32), jax.ShapeDtypeStruct((n_tiles * lr, d // 2), U32),
                   jax.ShapeDtypeStruct((t, LANES), F32), jax.ShapeDtypeStruct((n_tiles, 1, LANES), F32)],
        compiler_params=_cparams("parallel"),
        name="proj_router",
    )(x2, *acts, *ws, bias, g, wr, br, tri, ltri)


def _group_copies(table_ref, base, n, src_ref, dst_ref, sem):
    def issue(q, carry):
        g = table_ref[base + q]
        pltpu.make_async_copy(src_ref.at[pl.ds(pl.multiple_of(g * RUN_ROWS, RUN_ROWS), RUN_ROWS)],
                              dst_ref.at[pl.ds(pl.multiple_of(q * RUN_ROWS, RUN_ROWS), RUN_ROWS)], sem).start()
        return carry
    lax.fori_loop(0, n, issue, 0, unroll=8)


def _experts_kernel(be_ref, nused_ref, src_ref, xs_ref, wgu_ref, wd_ref, y_ref, xbuf, sems, wgu_b, wd_b):
    i = pl.program_id(0)
    n_used = nused_ref[0]
    rows = y_ref.shape[0]
    gpb = rows // RUN_ROWS
    slot = i % 2

    def fetch(blk, s):
        _group_copies(src_ref, blk * gpb, gpb, xs_ref, xbuf.at[s], sems.at[s])

    @pl.when(i == 0)
    def _():
        fetch(0, 0)

    @pl.when(i + 1 < n_used)
    def _():
        fetch(i + 1, 1 - slot)

    @pl.when((i == 0) | (be_ref[i] != be_ref[jnp.maximum(i - 1, 0)]))
    def _():
        wgu_b[...] = wgu_ref[...].astype(BF16)
        wd_b[...] = wd_ref[...].astype(BF16)

    @pl.when(i >= n_used)
    def _():
        y_ref[...] = jnp.zeros_like(y_ref)

    @pl.when(i < n_used)
    def _():
        pltpu.make_async_copy(xs_ref.at[pl.ds(0, rows)], xbuf.at[slot], sems.at[slot]).wait()
        ff = wd_b.shape[0]
        x = _unpack_halves(xbuf[slot]).astype(BF16)
        gu = _dot(x, wgu_b[...])
        h = (_silu(gu[:, :ff]) * gu[:, ff:]).astype(BF16)
        y_ref[...] = _pack_halves(_dot(h, wd_b[...]))


def _experts(block_expert, n_used, src_groups, xs, wgu, wd, layer, n_blocks):
    w = xs.shape[1]
    d, ff2 = wgu.shape[2:]
    rows = EXPERT_ROWS
    grid_spec = pltpu.PrefetchScalarGridSpec(
        num_scalar_prefetch=3,
        grid=(n_blocks,),
        in_specs=[pl.BlockSpec(memory_space=pl.ANY),
                  pl.BlockSpec((None, None, d, ff2), lambda i, be, nu, sg: (layer, be[i], 0, 0)),
                  pl.BlockSpec((None, None, ff2 // 2, d), lambda i, be, nu, sg: (layer, be[i], 0, 0))],
        out_specs=pl.BlockSpec((rows, w), lambda i, be, nu, sg: (i, 0)),
        scratch_shapes=[pltpu.VMEM((2, rows, w), U32), pltpu.SemaphoreType.DMA((2,)),
                        pltpu.VMEM((d, ff2), BF16), pltpu.VMEM((ff2 // 2, d), BF16)],
    )
    return pl.pallas_call(
        _experts_kernel,
        grid_spec=grid_spec,
        out_shape=jax.ShapeDtypeStruct((n_blocks * rows, w), U32),
        compiler_params=_cparams("arbitrary"),
        name="moe_experts",
    )(block_expert, n_used, src_groups, xs, wgu, wd)


def _combine_kernel(dst_ref, x_ref, gates_ref, y_ref, o_ref, ybuf, sems):
    i = pl.program_id(0)
    n_tiles = pl.num_programs(0)
    tm = x_ref.shape[0]
    lr = ybuf.shape[1]
    gpt = lr // RUN_ROWS
    slot = i % 2

    def fetch(tile, s):
        _group_copies(dst_ref, tile * gpt, gpt, y_ref, ybuf.at[s], sems.at[s])

    @pl.when(i == 0)
    def _():
        fetch(0, 0)

    @pl.when(i + 1 < n_tiles)
    def _():
        fetch(i + 1, 1 - slot)

    pltpu.make_async_copy(y_ref.at[pl.ds(0, lr)], ybuf.at[slot], sems.at[slot]).wait()
    ys = _unpack_halves(ybuf[slot]).astype(BF16)
    table = gates_ref[...]
    col = lax.broadcasted_iota(I32, (tm, lr), 1)
    moe = None
    for k in range(MOE_TOP_K):
        jk = table[:, MOE_TOP_K + k:MOE_TOP_K + k + 1].astype(I32)
        pick = jnp.where(col == jk, 1.0, 0.0).astype(BF16)
        term = table[:, k:k + 1] * _dot(pick, ys)
        moe = term if moe is None else moe + term
    o_ref[...] = x_ref[...] + moe


def _combine(dst_groups, xnew, gates, y, lr):
    t, d = xnew.shape
    tm = ROW_TILE
    grid_spec = pltpu.PrefetchScalarGridSpec(
        num_scalar_prefetch=1,
        grid=(t // tm,),
        in_specs=[pl.BlockSpec((tm, d), lambda i, p: (i, 0)),
                  pl.BlockSpec((tm, LANES), lambda i, p: (i, 0)),
                  pl.BlockSpec(memory_space=pl.ANY)],
        out_specs=pl.BlockSpec((tm, d), lambda i, p: (i, 0)),
        scratch_shapes=[pltpu.VMEM((2, lr, y.shape[1]), U32), pltpu.SemaphoreType.DMA((2,))],
    )
    return pl.pallas_call(
        _combine_kernel,
        grid_spec=grid_spec,
        out_shape=jax.ShapeDtypeStruct((t, d), F32),
        compiler_params=_cparams("arbitrary"),
        name="moe_combine",
    )(dst_groups, xnew, gates, y)


def _moe_plan(counts, n_groups, n_experts, lr):
    n_tiles = counts.shape[0]
    gpb = EXPERT_ROWS // RUN_ROWS
    gpt = lr // RUN_ROWS
    cnt = counts[:, 0, n_groups:n_groups + n_experts].astype(I32)
    run = (cnt + RUN_ROWS - 1) // RUN_ROWS
    run_end = jnp.cumsum(run, axis=1)
    run_off = run_end - run
    used = run_end[:, -1]
    seg = jnp.sum(run, axis=0)
    seg_pad = (seg + gpb - 1) // gpb * gpb
    seg_end = jnp.cumsum(seg_pad)
    seg_start = seg_end - seg_pad
    before = jnp.cumsum(run, axis=0) - run
    n_blocks = (n_tiles * gpt + n_experts * (gpb - 1)) // gpb + 1
    n_used = (seg_end[-1] // gpb).astype(I32)
    blk = jnp.arange(n_blocks, dtype=I32)
    first = jnp.minimum(blk, n_used - 1) * gpb
    block_expert = jnp.minimum(jnp.sum(seg_end[None, :] <= first[:, None], axis=1), n_experts - 1).astype(I32)

    zero_dst = n_blocks * gpb - 1
    q = jnp.arange(gpt, dtype=I32)
    owner = jnp.sum(run_end[:, None, :] <= q[None, :, None], axis=2)
    own = jax.nn.one_hot(jnp.minimum(owner, n_experts - 1), n_experts, dtype=I32)
    shift = seg_start[None, :] + before - run_off
    dst = q[None, :] + jnp.sum(own * shift[:, None, :], axis=2)
    dst = jnp.where(q[None, :] < used[:, None], dst, zero_dst).astype(I32).reshape(-1)

    pick = lambda onehot, tab: jnp.dot(onehot, tab.astype(F32), precision=lax.Precision.HIGHEST)
    zero_src = gpt - 1
    g = jnp.arange(n_blocks * gpb, dtype=I32)
    ge = jnp.minimum(jnp.sum(seg_end[None, :] <= g[:, None], axis=1), n_experts - 1)
    oh_e = (ge[:, None] == jnp.arange(n_experts, dtype=I32)[None, :]).astype(F32)
    per_e = pick(oh_e, jnp.stack([seg_start, seg], axis=1))
    m = g.astype(F32) - per_e[:, 0]
    tile_end = pick(oh_e, jnp.cumsum(run, axis=0).T)
    tile = jnp.minimum(jnp.sum(tile_end <= m[:, None], axis=1), n_tiles - 1)
    oh_t = (tile[:, None] == jnp.arange(n_tiles, dtype=I32)[None, :]).astype(F32)
    tile_base = (jnp.arange(n_tiles, dtype=I32)[:, None] * gpt + run_off - before).T
    src = jnp.sum(oh_t * pick(oh_e, tile_base), axis=1) + m
    src = jnp.where(m < per_e[:, 1], src, float(zero_src)).astype(I32)
    return block_expert, n_used.reshape(1), src, dst, n_blocks


def _moe(xnew, xs, gates, counts, wgu, wd, layer, n_groups):
    n_experts = wgu.shape[1]
    lr = _sorted_rows(ROW_TILE, n_experts)
    block_expert, n_used, src, dst, n_blocks = _moe_plan(counts, n_groups, n_experts, lr)
    y = _experts(block_expert, n_used, src, xs, wgu, wd, layer, n_blocks)
    return _combine(dst, xnew, gates, y, lr)


def _pw1_kernel(x_ref, g_ref, w_ref, b_ref, u_ref):
    ch = u_ref.shape[1]
    xn = _rms(x_ref[...], g_ref[...]).astype(BF16)
    y = _dot(xn, w_ref[...]) + b_ref[...]
    u_ref[...] = (y[:, :ch] * jax.nn.sigmoid(y[:, ch:])).astype(BF16)


def _pw1(x2, g, w, b):
    t, d = x2.shape
    tm = ROW_TILE
    ch = w.shape[1] // 2
    full = lambda a: pl.BlockSpec(a.shape, lambda i: (0,) * a.ndim)
    return pl.pallas_call(
        _pw1_kernel,
        grid=(t // tm,),
        in_specs=[pl.BlockSpec((tm, d), lambda i: (i, 0)), full(g), full(w), full(b)],
        out_specs=pl.BlockSpec((tm, ch), lambda i: (i, 0)),
        out_shape=jax.ShapeDtypeStruct((t, ch), BF16),
        compiler_params=_cparams("parallel"),
        name="pw1_glu",
    )(x2, g, w, b)


def _dwconv_kernel(cur_ref, halo_ref, w_ref, b_ref, lg_ref, lb_ref, o_ref, xs_ref, acc_ref):
    ts, ch = cur_ref.shape
    n_strip, n_tap = w_ref.shape[:2]
    n_chunk = ts // CONV_ROWS
    first_tap = CONV_HALO - (n_tap - 1)
    halo = halo_ref[...].astype(F32)
    halo = jnp.where(pl.program_id(1) > 0, halo, jnp.zeros_like(halo))
    cur = cur_ref[...].astype(F32)
    shifted_rows = ts + CONV_HALO - SUBLANES
    for c in range(n_strip):
        cs = slice(c * LANES, (c + 1) * LANES)
        xs_ref[0, c, :CONV_HALO, :] = halo[:, cs]
        xs_ref[0, c, CONV_HALO:, :] = cur[:, cs]
        for s in range(1, SUBLANES):
            xs_ref[s, c, :shifted_rows, :] = xs_ref[0, c, s:s + shifted_rows, :]

    def strip_chunk(idx, carry):
        r = idx // n_strip
        c = idx % n_strip
        row0 = pl.multiple_of(r * CONV_ROWS, CONV_ROWS)
        acc = jnp.broadcast_to(b_ref[c], (CONV_ROWS, LANES))
        for s in range(SUBLANES):
            taps = [k for k in range(n_tap) if (first_tap + k) % SUBLANES == s]
            lo = (first_tap + taps[0]) // SUBLANES * SUBLANES
            hi = (first_tap + taps[-1]) // SUBLANES * SUBLANES
            data = xs_ref[s, c, pl.ds(row0 + lo, hi - lo + CONV_ROWS), :]
            for k in taps:
                a = (first_tap + k) // SUBLANES * SUBLANES - lo
                acc = acc + w_ref[c, k:k + 1, :] * data[a:a + CONV_ROWS, :]
        acc_ref[c, pl.ds(row0, CONV_ROWS), :] = acc
        return carry

    lax.fori_loop(0, n_chunk * n_strip, strip_chunk, 0, unroll=2)
    y = jnp.concatenate([acc_ref[c] for c in range(n_strip)], axis=1)
    mu = jnp.mean(y, axis=-1, keepdims=True)
    cen = y - mu
    var = jnp.mean(cen * cen, axis=-1, keepdims=True)
    v = cen * lax.rsqrt(var + LN_EPS) * lg_ref[...] + lb_ref[...]
    o_ref[...] = _silu(v).astype(BF16)


def _dwconv(u, w, b, lg, lb, bsz, seq):
    ts = CONV_TILE
    nt = seq // ts
    ch = u.shape[1]
    hpt = ts // CONV_HALO
    n_strip = ch // LANES
    full = lambda a: pl.BlockSpec(a.shape, lambda bb, i: (0,) * a.ndim)
    w = w.reshape(-1, n_strip, LANES).transpose(1, 0, 2)
    b = b.reshape(n_strip, 1, LANES)
    return pl.pallas_call(
        _dwconv_kernel,
        grid=(bsz, nt),
        in_specs=[pl.BlockSpec((ts, ch), lambda bb, i: (bb * nt + i, 0)),
                  pl.BlockSpec((CONV_HALO, ch), lambda bb, i: (jnp.maximum((bb * nt + i) * hpt - 1, 0), 0)),
                  full(w), full(b), full(lg), full(lb)],
        out_specs=pl.BlockSpec((ts, ch), lambda bb, i: (bb * nt + i, 0)),
        out_shape=jax.ShapeDtypeStruct(u.shape, BF16),
        scratch_shapes=[pltpu.VMEM((SUBLANES, n_strip, CONV_HALO + ts, LANES), F32),
                        pltpu.VMEM((n_strip, ts, LANES), F32)],
        compiler_params=_cparams("parallel", "parallel"),
        name="dwconv_ln",
    )(u, u, w, b, lg, lb)


def _row(v):
    return v.reshape(1, -1).astype(F32)


def _pad_lanes(v, n=LANES):
    v = v.reshape(1, -1).astype(F32)
    return jnp.pad(v, ((0, 0), (0, n - v.shape[1])))


def _router_params(w_group, b_group, w_expert, b_expert):
    wr = jnp.concatenate([w_group, w_expert], axis=1)
    wr = jnp.pad(wr, ((0, 0), (0, LANES - wr.shape[1]))).astype(BF16)
    br = _pad_lanes(jnp.concatenate([b_group.reshape(-1), b_expert.reshape(-1)]))
    return wr, br


def kernel(x, mix_norm_g, w_in, q_norm_g, k_norm_g, attn_sinks, ssm_conv_w, ssm_conv_b, ssm_dt_bias, ssm_a_log, ssm_d, ssm_out_norm_g, w_out, conv_norm_g, conv_w_pw1, conv_b_pw1, conv_w_dw, conv_b_dw, conv_ln_g, conv_ln_b, conv_w_pw2, conv_b_pw2, moe_norm_g, moe_w_group, moe_b_group, moe_w_expert, moe_b_expert, moe_w_gate_up, moe_w_down):
    bsz, seq, d = x.shape
    t = bsz * seq
    n_heads = attn_sinks.shape[1]
    n_kv = n_heads // GQA_REP
    q_cols = n_heads * HEAD_DIM
    kv_cols = n_kv * HEAD_DIM
    d_inner = ssm_out_norm_g.shape[1]
    ssm_heads = ssm_a_log.shape[1]
    n_groups = moe_w_group.shape[2]
    n_experts = moe_w_expert.shape[2]
    assert t % ROW_TILE == 0 and seq % CONV_TILE == 0 and seq % (SSM_CHUNK * SSD_STEP_CHUNKS) == 0 and seq % (ATTN_BLOCK * ATTN_STEP_BLOCKS) == 0
    assert ssm_heads <= LANES and n_groups + n_experts <= LANES

    x2 = x.reshape(t, d)
    tri_strict = jnp.asarray(np.tril(np.ones((ROW_TILE, ROW_TILE), np.float32), -1), BF16)
    lane_before = jnp.asarray(np.triu(np.ones((LANES, LANES), np.float32), 1), BF16)
    zero_bias = jnp.zeros((1, d), F32)

    w = w_in[0]
    cuts = [0] + np.cumsum([q_cols, kv_cols, kv_cols, d_inner, ssm_conv_w.shape[2]]).tolist() + [w.shape[1]]
    wq, wk, wv, wz, wx, wdt = (w[:, lo:hi] for lo, hi in zip(cuts[:-1], cuts[1:]))
    wdt = jnp.pad(wdt, ((0, 0), (0, LANES - ssm_heads)))
    seg_ones = jnp.asarray(np.kron(np.eye(q_cols // HEAD_DIM, dtype=np.float32),
                                   np.ones((HEAD_DIM, HEAD_DIM), np.float32)), BF16)
    qg = jnp.tile(_row(q_norm_g[0]), (1, n_heads))
    kg = jnp.tile(_row(k_norm_g[0]), (1, n_kv))
    q, k, v, z, xbc, dt = _inproj(
        x2, _row(mix_norm_g[0]), wq.astype(BF16), wk.astype(BF16), wv.astype(BF16), wz.astype(BF16),
        wx.astype(BF16), wdt.astype(BF16), qg, kg, seg_ones)
    kv_rep = jnp.asarray(np.stack([np.kron(np.eye(n_kv, dtype=np.float32)[:, g:g + 1],
                                           np.tile(np.eye(HEAD_DIM, dtype=np.float32), (1, GQA_REP)))
                                   for g in range(n_kv)]), BF16)
    y_attn = _attention(attn_sinks[0].reshape(1, n_heads).astype(F32), kv_rep, q, k, v, bsz, seq)
    expand = jnp.asarray(np.kron(np.eye(LANES, ssm_heads, dtype=np.float32),
                                 np.ones((1, SSM_HEAD_DIM), np.float32)), BF16)
    tri_incl = jnp.asarray(np.tril(np.ones((SSM_CHUNK, SSM_CHUNK), np.float32)), BF16)
    n_tap = ssm_conv_w.shape[1]
    shift = jnp.asarray(np.concatenate([np.eye(SSM_CHUNK, 2 * SSM_CHUNK, SSM_CHUNK - j, dtype=np.float32)
                                        for j in range(1, n_tap)], axis=0), BF16)
    y_ssm = _ssd(xbc, z, dt, ssm_conv_w[0].astype(F32), _row(ssm_conv_b[0]), _pad_lanes(ssm_dt_bias[0]),
                 _pad_lanes(ssm_a_log[0]), jnp.repeat(_row(ssm_d[0]), SSM_HEAD_DIM, axis=1), _row(ssm_out_norm_g[0]),
                 expand, tri_incl, shift, bsz, seq)
    wo = w_out[0].astype(BF16)
    wr, br = _router_params(moe_w_group[0], moe_b_group[0], moe_w_expert[0], moe_b_expert[0])
    xnew, xs, gates, counts = _proj_router(
        x2, [y_attn, y_ssm], [wo[:q_cols], wo[q_cols:]], zero_bias, _row(moe_norm_g[0]), wr, br, tri_strict,
        lane_before, n_groups, n_experts)
    x2 = _moe(xnew, xs, gates, counts, moe_w_gate_up, moe_w_down, 0, n_groups)

    u = _pw1(x2, _row(conv_norm_g[0]), conv_w_pw1[0].astype(BF16), _row(conv_b_pw1[0]))
    u = _dwconv(u, conv_w_dw[0].astype(F32), _row(conv_b_dw[0]), _row(conv_ln_g[0]), _row(conv_ln_b[0]), bsz, seq)
    wr, br = _router_params(moe_w_group[1], moe_b_group[1], moe_w_expert[1], moe_b_expert[1])
    xnew, xs, gates, counts = _proj_router(
        x2, [u], [conv_w_pw2[0].astype(BF16)], _row(conv_b_pw2[0]), _row(moe_norm_g[1]), wr, br, tri_strict,
        lane_before, n_groups, n_experts)
    x2 = _moe(xnew, xs, gates, counts, moe_w_gate_up, moe_w_down, 1, n_groups)
    return x2.reshape(bsz, seq, d)
```

```python
import functools

import numpy as np
import jax
import jax.numpy as jnp
from jax import lax
from jax.experimental import pallas as pl
from jax.experimental.pallas import tpu as pltpu

F32 = jnp.float32
BF16 = jnp.bfloat16
I32 = jnp.int32
U32 = jnp.uint32

HEAD_DIM = 64
GQA_REP = 4
ATTN_BLOCK = 128
SSM_HEAD_DIM = 64
SSM_GROUPS = 2
SSM_D_STATE = 128
SSM_CHUNK = 128
MOE_TOP_K = 2
RMS_EPS = 1e-6
LN_EPS = 1e-5

LANES = 128
SUBLANES = 8

ROW_TILE = 512
EXPERT_ROWS = 512
SSD_STEP_CHUNKS = 2
ATTN_STEP_BLOCKS = 2
CONV_TILE = 512
CONV_HALO = 32
CONV_ROWS = 64
RUN_ROWS = SUBLANES
VMEM_LIMIT = 48 * 1024 * 1024


def _cparams(*sem):
    return pltpu.CompilerParams(dimension_semantics=sem, vmem_limit_bytes=VMEM_LIMIT)


def _dot(a, b):
    return jnp.dot(a, b, preferred_element_type=F32)


def _rms(x, g):
    return x * lax.rsqrt(jnp.mean(x * x, axis=-1, keepdims=True) + RMS_EPS) * g


def _silu(x):
    return x * jax.nn.sigmoid(x)


def _pack_halves(y, is_bf16=False):
    c = y.shape[1] // 2
    if is_bf16:
        return lax.bitcast_convert_type(y[:, :c], U32) | (lax.bitcast_convert_type(y[:, c:], U32) >> 16)
    hi = lax.bitcast_convert_type(y[:, :c].astype(BF16).astype(F32), U32)
    lo = lax.bitcast_convert_type(y[:, c:].astype(BF16).astype(F32), U32)
    return (hi & jnp.uint32(0xFFFF0000)) | (lo >> 16)


def _unpack_halves(u):
    hi = lax.bitcast_convert_type(u & jnp.uint32(0xFFFF0000), F32)
    lo = lax.bitcast_convert_type(u << 16, F32)
    return jnp.concatenate([hi, lo], axis=1)


def _inproj_kernel(x_ref, g_ref, wq_ref, wk_ref, wv_ref, wz_ref, wx_ref, wdt_ref, qg_ref, kg_ref, bd_ref,
                   q_out, k_out, v_out, z_out, xbc_out, dt_out):
    xn = _rms(x_ref[...], g_ref[...]).astype(BF16)

    def head_rms(y, gain):
        bd = bd_ref[:y.shape[1], :y.shape[1]]
        sq = y * y
        hi = sq.astype(BF16)
        lo = (sq - hi.astype(F32)).astype(BF16)
        ss = _dot(hi, bd) + _dot(lo, bd)
        return y * lax.rsqrt(ss * (1.0 / HEAD_DIM) + RMS_EPS) * gain

    q = head_rms(_dot(xn, wq_ref[...]), qg_ref[...])
    q_out[...] = (q * (HEAD_DIM ** -0.5)).astype(BF16)
    k_out[...] = head_rms(_dot(xn, wk_ref[...]), kg_ref[...]).astype(BF16)
    v_out[...] = _dot(xn, wv_ref[...]).astype(BF16)
    z_out[...] = _dot(xn, wz_ref[...]).astype(BF16)
    xbc_out[...] = _dot(xn, wx_ref[...]).astype(BF16)
    dt_out[...] = _dot(xn, wdt_ref[...])


def _inproj(x2, g, wq, wk, wv, wz, wx, wdt, qg, kg, bd):
    t, d = x2.shape
    tm = ROW_TILE
    row = lambda n: pl.BlockSpec((tm, n), lambda i: (i, 0))
    full = lambda a: pl.BlockSpec(a.shape, lambda i: (0,) * a.ndim)
    outs = [(wq.shape[1], BF16), (wk.shape[1], BF16), (wv.shape[1], BF16), (wz.shape[1], BF16),
            (wx.shape[1], BF16), (wdt.shape[1], F32)]
    return pl.pallas_call(
        _inproj_kernel,
        grid=(t // tm,),
        in_specs=[row(d)] + [full(a) for a in (g, wq, wk, wv, wz, wx, wdt, qg, kg, bd)],
        out_specs=[row(n) for n, _ in outs],
        out_shape=[jax.ShapeDtypeStruct((t, n), dt) for n, dt in outs],
        compiler_params=_cparams("parallel"),
        name="inproj",
    )(x2, g, wq, wk, wv, wz, wx, wdt, qg, kg, bd)


def _attn_kernel(sink_ref, rep_ref, q_ref, kc_ref, kp_ref, vc_ref, vp_ref, o_ref):
    n = pl.program_id(1)
    blk = ATTN_BLOCK
    gw = GQA_REP * HEAD_DIM
    n_kv = q_ref.shape[1] // gw
    qi = lax.broadcasted_iota(I32, (blk, 2 * blk), 0)
    kj = lax.broadcasted_iota(I32, (blk, 2 * blk), 1)
    rel = qi + blk - kj
    in_window = (rel >= 0) & (rel < ATTN_BLOCK)
    first_key = jnp.where(n > 0, 0, blk)
    key_head = lax.broadcasted_iota(I32, (2 * blk, gw), 1) // HEAD_DIM
    out_head = lax.broadcasted_iota(I32, (blk, gw), 1) // HEAD_DIM
    for sb, g in [(sb, g) for sb in range(ATTN_STEP_BLOCKS) for g in range(n_kv)]:
        sl = slice(g * gw, (g + 1) * gw)
        rows = slice(sb * blk, (sb + 1) * blk)
        before = slice((sb - 1) * blk, sb * blk)
        band = in_window & (kj >= first_key) if sb == 0 else in_window
        qg = q_ref[rows, sl]
        kcat = jnp.concatenate([kp_ref[...] if sb == 0 else kc_ref[before, :], kc_ref[rows, :]], axis=0)
        vcat = jnp.concatenate([vp_ref[...] if sb == 0 else vc_ref[before, :], vc_ref[rows, :]], axis=0)
        kk = _dot(kcat, rep_ref[g]).astype(BF16)
        vv = _dot(vcat, rep_ref[g]).astype(BF16)
        zero = jnp.zeros_like(kk)
        kbd = jnp.concatenate([jnp.where(key_head == h, kk, zero) for h in range(GQA_REP)], axis=0)
        vbd = jnp.concatenate([jnp.where(key_head == h, vv, zero) for h in range(GQA_REP)], axis=0)
        s = lax.dot_general(qg, kbd, (((1,), (1,)), ((), ())), preferred_element_type=F32)
        probs, scale = [], None
        for h in range(GQA_REP):
            sh = jnp.where(band, s[:, h * 2 * blk:(h + 1) * 2 * blk], -jnp.inf)
            sink = sink_ref[0, g * GQA_REP + h]
            m = jnp.maximum(jnp.max(sh, axis=-1, keepdims=True), sink)
            p = jnp.exp(sh - m)
            denom = jnp.sum(p, axis=-1, keepdims=True) + jnp.exp(sink - m)
            probs.append(p.astype(BF16))
            inv = jnp.broadcast_to(1.0 / denom, (blk, gw))
            scale = inv if scale is None else jnp.where(out_head == h, inv, scale)
        o = _dot(jnp.concatenate(probs, axis=1), vbd)
        o_ref[rows, sl] = (o * scale).astype(BF16)


def _attention(sinks, kv_rep, q, k, v, bsz, seq):
    blk = ATTN_BLOCK
    spb = ATTN_STEP_BLOCKS
    nb = seq // (blk * spb)
    cur = lambda w: pl.BlockSpec((blk * spb, w), lambda b, n: (b * nb + n, 0))
    prev = lambda w: pl.BlockSpec((blk, w), lambda b, n: (jnp.maximum((b * nb + n) * spb - 1, 0), 0))
    qw, kw = q.shape[1], k.shape[1]
    return pl.pallas_call(
        _attn_kernel,
        grid=(bsz, nb),
        in_specs=[pl.BlockSpec(memory_space=pltpu.SMEM), pl.BlockSpec(kv_rep.shape, lambda b, n: (0, 0, 0)),
                  cur(qw), cur(kw), prev(kw), cur(kw), prev(kw)],
        out_specs=cur(qw),
        out_shape=jax.ShapeDtypeStruct(q.shape, BF16),
        compiler_params=_cparams("parallel", "parallel"),
        name="swa_attention",
    )(sinks, kv_rep, q, k, k, v, v)


def _split3(a):
    a1 = a.astype(BF16)
    r = a - a1.astype(F32)
    a2 = r.astype(BF16)
    return a1, a2, (r - a2.astype(F32)).astype(BF16)


def _ssd_kernel(xbc_ref, z_ref, dt_ref, cw_ref, cb_ref, dtb_ref, alog_ref, dskip_ref, og_ref, expand_ref, tri_ref,
                shift_ref, y_ref, prev_ref, state_ref):
    lc = SSM_CHUNK

    @pl.when(pl.program_id(1) == 0)
    def _():
        prev_ref[...] = jnp.zeros_like(prev_ref)
        state_ref[...] = jnp.zeros_like(state_ref)

    for cc in range(SSD_STEP_CHUNKS):
        rows = slice(cc * lc, (cc + 1) * lc)
        prev = prev_ref[...] if cc == 0 else xbc_ref[(cc - 1) * lc:cc * lc, :]
        _ssd_chunk(xbc_ref[rows, :], prev, z_ref[rows, :], dt_ref[rows, :], cw_ref, cb_ref, dtb_ref, alog_ref,
                   dskip_ref, og_ref, expand_ref, tri_ref, shift_ref, y_ref.at[rows, :], state_ref)
    prev_ref[...] = xbc_ref[(SSD_STEP_CHUNKS - 1) * lc:, :]


def _ssd_chunk(xb, prev, z, dt, cw_ref, cb_ref, dtb_ref, alog_ref, dskip_ref, og_ref, expand_ref, tri_ref, shift_ref,
               y_ref, state_ref):
    lc = xb.shape[0]
    d_inner = z.shape[1]
    n_heads = d_inner // SSM_HEAD_DIM
    gn = SSM_GROUPS * SSM_D_STATE
    hpg = n_heads // SSM_GROUPS
    gw = hpg * SSM_HEAD_DIM
    n_tap = cw_ref.shape[0]

    sh = _dot(shift_ref[...], jnp.concatenate([prev, xb], axis=0))
    acc = cb_ref[...] + cw_ref[n_tap - 1:n_tap, :] * xb.astype(F32)
    for j in range(1, n_tap):
        acc = acc + cw_ref[n_tap - 1 - j:n_tap - j, :] * sh[(j - 1) * lc:j * lc, :]
    u = _silu(acc)
    xs = u[:, :d_inner]
    bm = u[:, d_inner:d_inner + gn].astype(BF16)
    cm = u[:, d_inner + gn:].astype(BF16)

    dtr = dt + dtb_ref[...]
    dt_c = jnp.maximum(dtr, 0.0) + jnp.log1p(jnp.exp(-jnp.abs(dtr)))
    a_c = dt_c * (-jnp.exp(alog_ref[...]))
    cum = _dot(tri_ref[...], jnp.concatenate(_split3(a_c), axis=1))
    acum_c = (cum[:, :LANES] + cum[:, LANES:2 * LANES]) + cum[:, 2 * LANES:]
    acum_r = acum_c.T
    ex = _dot(jnp.concatenate(_split3(dt_c) + _split3(acum_c), axis=0), expand_ref[...])
    dt_e = (ex[:lc] + ex[lc:2 * lc]) + ex[2 * lc:3 * lc]
    acum_e = (ex[3 * lc:4 * lc] + ex[4 * lc:5 * lc]) + ex[5 * lc:]
    xdt = xs * dt_e

    li = lax.broadcasted_iota(I32, (lc, lc), 0)
    si = lax.broadcasted_iota(I32, (lc, lc), 1)
    causal = li >= si
    mats = []
    for g in range(SSM_GROUPS):
        cb = lax.dot_general(cm[:, g * SSM_D_STATE:(g + 1) * SSM_D_STATE], bm[:, g * SSM_D_STATE:(g + 1) * SSM_D_STATE],
                             (((1,), (1,)), ((), ())), preferred_element_type=F32)
        for r in range(hpg):
            h = g * hpg + r
            seg = acum_c[:, h:h + 1] - acum_r[h:h + 1, :]
            decay = jnp.exp(jnp.where(causal, seg, -jnp.inf))
            mats.append((cb * decay).astype(BF16))
    xdt_b = xdt.astype(BF16)
    row_head = lax.broadcasted_iota(I32, (lc, d_inner), 1) // SSM_HEAD_DIM
    zero = jnp.zeros_like(xdt_b)
    xbd = jnp.concatenate([jnp.where(row_head == h, xdt_b, zero) for h in range(n_heads)], axis=0)
    y = _dot(jnp.concatenate(mats, axis=1), xbd)

    a_last = acum_e[lc - 1:lc, :]
    w_state = (xdt * jnp.exp(a_last - acum_e)).astype(BF16)
    y_off = []
    for g in range(SSM_GROUPS):
        ns = slice(g * SSM_D_STATE, (g + 1) * SSM_D_STATE)
        hs = slice(g * gw, (g + 1) * gw)
        h_in = state_ref[:, hs]
        y_off.append(_dot(cm[:, ns], h_in.astype(BF16)))
        new = lax.dot_general(bm[:, ns], w_state[:, hs], (((0,), (0,)), ((), ())), preferred_element_type=F32)
        state_ref[:, hs] = h_in * jnp.exp(a_last[:, hs]) + new
    y = y + jnp.concatenate(y_off, axis=1) * jnp.exp(acum_e)

    y = y + dskip_ref[...] * xs
    y = y * _silu(z.astype(F32))
    y_ref[...] = _rms(y, og_ref[...]).astype(BF16)


def _ssd(xbc, z, dt, cw, cb, dtb, alog, dskip, og, expand, tri, shift, bsz, seq):
    rows = SSM_CHUNK * SSD_STEP_CHUNKS
    ns = seq // rows
    blk = lambda n: pl.BlockSpec((rows, n), lambda b, c: (b * ns + c, 0))
    full = lambda a: pl.BlockSpec(a.shape, lambda b, c: (0,) * a.ndim)
    d_inner = z.shape[1]
    return pl.pallas_call(
        _ssd_kernel,
        grid=(bsz, ns),
        in_specs=[blk(xbc.shape[1]), blk(d_inner), blk(dt.shape[1])]
        + [full(a) for a in (cw, cb, dtb, alog, dskip, og, expand, tri, shift)],
        out_specs=blk(d_inner),
        out_shape=jax.ShapeDtypeStruct(z.shape, BF16),
        scratch_shapes=[pltpu.VMEM((SSM_CHUNK, xbc.shape[1]), BF16), pltpu.VMEM((SSM_D_STATE, d_inner), F32)],
        compiler_params=_cparams("arbitrary", "arbitrary"),
        name="ssd_scan",
    )(xbc, z, dt, cw, cb, dtb, alog, dskip, og, expand, tri, shift)


def _sorted_rows(tm, n_experts):
    return MOE_TOP_K * tm + n_experts * RUN_ROWS


def _proj_router_kernel(n_act, n_groups, n_experts, *refs):
    x_ref = refs[0]
    act_refs = refs[1:1 + n_act]
    w_refs = refs[1 + n_act:1 + 2 * n_act]
    b_ref, g_ref, wr_ref, br_ref, tri_ref, ltri_ref = refs[1 + 2 * n_act:7 + 2 * n_act]
    xnew_ref, xs_ref, gates_ref, counts_ref = refs[7 + 2 * n_act:]
    tm = x_ref.shape[0]
    epg = n_experts // n_groups

    y = x_ref[...] + b_ref[...]
    for a_ref, w_ref in zip(act_refs, w_refs):
        y = y + _dot(a_ref[...], w_ref[...])
    xnew_ref[...] = y
    xn = _rms(y, g_ref[...]).astype(BF16)

    logits = _dot(xn, wr_ref[...]) + br_ref[...]
    lane = lax.broadcasted_iota(I32, logits.shape, 1)
    big = jnp.int32(LANES)
    neg = -jnp.inf

    def first_argmax(v):
        m = jnp.max(v, axis=-1, keepdims=True)
        return m, jnp.min(jnp.where(v == m, lane, big), axis=-1, keepdims=True)

    gl = jnp.where(lane < n_groups, logits, neg)
    gmax, gsel = first_argmax(gl)
    g_w = 1.0 / jnp.sum(jnp.exp(gl - gmax), axis=-1, keepdims=True)
    elane = lane - n_groups
    in_group = (elane >= 0) & (elane < n_experts) & ((elane // epg) == gsel)
    el = jnp.where(in_group, logits, neg)
    top1, i1 = first_argmax(el)
    top2, i2 = first_argmax(jnp.where(lane == i1, neg, el))
    e2 = jnp.exp(top2 - top1)
    w1 = 1.0 / (1.0 + e2)
    gate1 = g_w * w1
    gate2 = g_w * (e2 * w1)

    sel1 = lane == i1
    sel2 = lane == i2
    onehot = jnp.where(sel1, 1.0, jnp.where(sel2, 1.0, 0.0))
    before = _dot(tri_ref[...], onehot.astype(BF16))
    cnt = jnp.sum(onehot, axis=0, keepdims=True)
    counts_ref[...] = cnt
    cnt_pad = jnp.floor((cnt + (RUN_ROWS - 1)) * (1.0 / RUN_ROWS)) * RUN_ROWS
    run_start = _dot(jnp.broadcast_to(cnt_pad, (SUBLANES, LANES)).astype(BF16), ltri_ref[...])[0:1, :]
    slot = run_start + before
    j1 = jnp.sum(jnp.where(sel1, slot, 0.0), axis=-1, keepdims=True)
    j2 = jnp.sum(jnp.where(sel2, slot, 0.0), axis=-1, keepdims=True)
    table = jnp.where(lane == 0, gate1, jnp.where(lane == 1, gate2,
                                                  jnp.where(lane == 2, j1, jnp.where(lane == 3, j2, 0.0))))
    gates_ref[...] = table

    tt = table.T
    j1r = tt[2:3, :].astype(I32)
    j2r = tt[3:4, :].astype(I32)
    ri = lax.broadcasted_iota(I32, (xs_ref.shape[0], tm), 0)
    perm = jnp.where(ri == j1r, 1.0, jnp.where(ri == j2r, 1.0, 0.0)).astype(BF16)
    xs_ref[...] = _pack_halves(_dot(perm, xn), is_bf16=True)


def _proj_router(x2, acts, ws, bias, g, wr, br, tri, ltri, n_groups, n_experts):
    t, d = x2.shape
    tm = ROW_TILE
    n_tiles = t // tm
    lr = _sorted_rows(tm, n_experts)
    row = lambda n: pl.BlockSpec((tm, n), lambda i: (i, 0))
    full = lambda a: pl.BlockSpec(a.shape, lambda i: (0,) * a.ndim)
    kern = functools.partial(_proj_router_kernel, len(acts), n_groups, n_experts)
    return pl.pallas_call(
        kern,
        grid=(n_tiles,),
        in_specs=[row(d)] + [row(a.shape[1]) for a in acts] + [full(a) for a in (*ws, bias, g, wr, br, tri, ltri)],
        out_specs=[row(d), pl.BlockSpec((lr, d // 2), lambda i: (i, 0)), row(LANES),
                   pl.BlockSpec((None, 1, LANES), lambda i: (i, 0, 0))],
        out_shape=[jax.ShapeDtypeStruct((t, d), F32), jax.ShapeDtypeStruct((n_tiles * lr, d // 2), U32),
                   jax.ShapeDtypeStruct((t, LANES), F32), jax.ShapeDtypeStruct((n_tiles, 1, LANES), F32)],
        compiler_params=_cparams("parallel"),
        name="proj_router",
    )(x2, *acts, *ws, bias, g, wr, br, tri, ltri)


def _group_copies(table_ref, base, n, src_ref, dst_ref, sem):
    for q in range(n):
        g = table_ref[base + q]
        pltpu.make_async_copy(src_ref.at[pl.ds(pl.multiple_of(g * RUN_ROWS, RUN_ROWS), RUN_ROWS)],
                              dst_ref.at[pl.ds(q * RUN_ROWS, RUN_ROWS)], sem).start()


def _experts_kernel(be_ref, nused_ref, src_ref, xs_ref, wgu_ref, wd_ref, y_ref, xbuf, sems, wgu_b, wd_b):
    i = pl.program_id(0)
    n_used = nused_ref[0]
    rows = y_ref.shape[0]
    gpb = rows // RUN_ROWS
    slot = i % 2

    def fetch(blk, s):
        _group_copies(src_ref, blk * gpb, gpb, xs_ref, xbuf.at[s], sems.at[s])

    def wait(s):
        pltpu.make_async_copy(xs_ref.at[pl.ds(0, rows)], xbuf.at[s], sems.at[s]).wait()

    @pl.when(i == 0)
    def _():
        fetch(0, 0)

    @pl.when((i == 0) | (be_ref[i] != be_ref[jnp.maximum(i - 1, 0)]))
    def _():
        wgu_b[...] = wgu_ref[...].astype(BF16)
        wd_b[...] = wd_ref[...].astype(BF16)

    @pl.when(i >= n_used)
    def _():
        y_ref[...] = jnp.zeros_like(y_ref)

    @pl.when(i == n_used)
    def _():
        wait(slot)

    @pl.when(i < n_used)
    def _():
        wait(slot)
        fetch(i + 1, 1 - slot)
        ff = wd_b.shape[0]
        x = _unpack_halves(xbuf[slot]).astype(BF16)
        gu = _dot(x, wgu_b[...])
        h = (_silu(gu[:, :ff]) * gu[:, ff:]).astype(BF16)
        y_ref[...] = _pack_halves(_dot(h, wd_b[...]))


def _experts(block_expert, n_used, src_groups, xs, wgu, wd, layer, n_blocks):
    w = xs.shape[1]
    d, ff2 = wgu.shape[2:]
    rows = EXPERT_ROWS
    grid_spec = pltpu.PrefetchScalarGridSpec(
        num_scalar_prefetch=3,
        grid=(n_blocks,),
        in_specs=[pl.BlockSpec(memory_space=pl.ANY),
                  pl.BlockSpec((None, None, d, ff2), lambda i, be, nu, sg: (layer, be[i], 0, 0)),
                  pl.BlockSpec((None, None, ff2 // 2, d), lambda i, be, nu, sg: (layer, be[i], 0, 0))],
        out_specs=pl.BlockSpec((rows, w), lambda i, be, nu, sg: (i, 0)),
        scratch_shapes=[pltpu.VMEM((2, rows, w), U32), pltpu.SemaphoreType.DMA((2,)),
                        pltpu.VMEM((d, ff2), BF16), pltpu.VMEM((ff2 // 2, d), BF16)],
    )
    return pl.pallas_call(
        _experts_kernel,
        grid_spec=grid_spec,
        out_shape=jax.ShapeDtypeStruct((n_blocks * rows, w), U32),
        compiler_params=_cparams("arbitrary"),
        name="moe_experts",
    )(block_expert, n_used, src_groups, xs, wgu, wd)


def _combine_kernel(dst_ref, x_ref, gates_ref, y_ref, o_ref, ybuf, sems):
    i = pl.program_id(0)
    n_tiles = pl.num_programs(0)
    tm = x_ref.shape[0]
    lr = ybuf.shape[1]
    gpt = lr // RUN_ROWS
    slot = i % 2

    def fetch(tile, s):
        _group_copies(dst_ref, tile * gpt, gpt, y_ref, ybuf.at[s], sems.at[s])

    def wait(s):
        pltpu.make_async_copy(y_ref.at[pl.ds(0, lr)], ybuf.at[s], sems.at[s]).wait()

    @pl.when(i == 0)
    def _():
        fetch(0, 0)

    wait(slot)
    fetch(jnp.minimum(i + 1, n_tiles - 1), 1 - slot)
    ys = _unpack_halves(ybuf[slot]).astype(BF16)
    table = gates_ref[...]
    col = lax.broadcasted_iota(I32, (tm, lr), 1)
    pick = jnp.zeros((tm, lr), F32)
    for k in range(MOE_TOP_K):
        jk = table[:, MOE_TOP_K + k:MOE_TOP_K + k + 1].astype(I32)
        pick = jnp.where(col == jk, table[:, k:k + 1], pick)
    o_ref[...] = x_ref[...] + _dot(pick.astype(BF16), ys)

    @pl.when(i == n_tiles - 1)
    def _():
        wait(1 - slot)


def _combine(dst_groups, xnew, gates, y, lr):
    t, d = xnew.shape
    tm = ROW_TILE
    grid_spec = pltpu.PrefetchScalarGridSpec(
        num_scalar_prefetch=1,
        grid=(t // tm,),
        in_specs=[pl.BlockSpec((tm, d), lambda i, p: (i, 0)),
                  pl.BlockSpec((tm, LANES), lambda i, p: (i, 0)),
                  pl.BlockSpec(memory_space=pl.ANY)],
        out_specs=pl.BlockSpec((tm, d), lambda i, p: (i, 0)),
        scratch_shapes=[pltpu.VMEM((2, lr, y.shape[1]), U32), pltpu.SemaphoreType.DMA((2,))],
    )
    return pl.pallas_call(
        _combine_kernel,
        grid_spec=grid_spec,
        out_shape=jax.ShapeDtypeStruct((t, d), F32),
        compiler_params=_cparams("arbitrary"),
        name="moe_combine",
    )(dst_groups, xnew, gates, y)


def _moe_plan(counts, n_groups, n_experts, lr):
    n_tiles = counts.shape[0]
    gpb = EXPERT_ROWS // RUN_ROWS
    gpt = lr // RUN_ROWS
    cnt = counts[:, 0, n_groups:n_groups + n_experts].astype(I32)
    run = (cnt + RUN_ROWS - 1) // RUN_ROWS
    run_end = jnp.cumsum(run, axis=1)
    run_off = run_end - run
    used = run_end[:, -1]
    seg = jnp.sum(run, axis=0)
    seg_pad = (seg + gpb - 1) // gpb * gpb
    seg_end = jnp.cumsum(seg_pad)
    seg_start = seg_end - seg_pad
    before = jnp.cumsum(run, axis=0) - run
    n_blocks = (n_tiles * gpt + n_experts * (gpb - 1)) // gpb + 1
    n_used = (seg_end[-1] // gpb).astype(I32)
    blk = jnp.arange(n_blocks, dtype=I32)
    first = jnp.minimum(blk, n_used - 1) * gpb
    block_expert = jnp.minimum(jnp.sum(seg_end[None, :] <= first[:, None], axis=1), n_experts - 1).astype(I32)

    zero_dst = n_blocks * gpb - 1
    q = jnp.arange(gpt, dtype=I32)
    owner = jnp.sum(run_end[:, None, :] <= q[None, :, None], axis=2)
    own = jax.nn.one_hot(jnp.minimum(owner, n_experts - 1), n_experts, dtype=I32)
    shift = seg_start[None, :] + before - run_off
    dst = q[None, :] + jnp.sum(own * shift[:, None, :], axis=2)
    dst = jnp.where(q[None, :] < used[:, None], dst, zero_dst).astype(I32).reshape(-1)

    pick = lambda onehot, tab: jnp.dot(onehot, tab.astype(F32), precision=lax.Precision.HIGHEST)
    zero_src = gpt - 1
    g = jnp.arange(n_blocks * gpb, dtype=I32)
    ge = jnp.minimum(jnp.sum(seg_end[None, :] <= g[:, None], axis=1), n_experts - 1)
    oh_e = (ge[:, None] == jnp.arange(n_experts, dtype=I32)[None, :]).astype(F32)
    per_e = pick(oh_e, jnp.stack([seg_start, seg], axis=1))
    m = g.astype(F32) - per_e[:, 0]
    tile_end = pick(oh_e, jnp.cumsum(run, axis=0).T)
    tile = jnp.minimum(jnp.sum(tile_end <= m[:, None], axis=1), n_tiles - 1)
    oh_t = (tile[:, None] == jnp.arange(n_tiles, dtype=I32)[None, :]).astype(F32)
    tile_base = (jnp.arange(n_tiles, dtype=I32)[:, None] * gpt + run_off - before).T
    src = jnp.sum(oh_t * pick(oh_e, tile_base), axis=1) + m
    src = jnp.where(m < per_e[:, 1], src, float(zero_src)).astype(I32)
    return block_expert, n_used.reshape(1), src, dst, n_blocks


def _moe(xnew, xs, gates, counts, wgu, wd, layer, n_groups):
    n_experts = wgu.shape[1]
    lr = _sorted_rows(ROW_TILE, n_experts)
    block_expert, n_used, src, dst, n_blocks = _moe_plan(counts, n_groups, n_experts, lr)
    y = _experts(block_expert, n_used, src, xs, wgu, wd, layer, n_blocks)
    return _combine(dst, xnew, gates, y, lr)


def _pw1_kernel(x_ref, g_ref, w_ref, b_ref, u_ref):
    ch = u_ref.shape[1]
    xn = _rms(x_ref[...], g_ref[...]).astype(BF16)
    y = _dot(xn, w_ref[...]) + b_ref[...]
    u_ref[...] = (y[:, :ch] * jax.nn.sigmoid(y[:, ch:])).astype(BF16)


def _pw1(x2, g, w, b):
    t, d = x2.shape
    tm = ROW_TILE
    ch = w.shape[1] // 2
    full = lambda a: pl.BlockSpec(a.shape, lambda i: (0,) * a.ndim)
    return pl.pallas_call(
        _pw1_kernel,
        grid=(t // tm,),
        in_specs=[pl.BlockSpec((tm, d), lambda i: (i, 0)), full(g), full(w), full(b)],
        out_specs=pl.BlockSpec((tm, ch), lambda i: (i, 0)),
        out_shape=jax.ShapeDtypeStruct((t, ch), BF16),
        compiler_params=_cparams("parallel"),
        name="pw1_glu",
    )(x2, g, w, b)


def _dwconv_kernel(cur_ref, halo_ref, w_ref, b_ref, lg_ref, lb_ref, o_ref, xs_ref, acc_ref):
    ts, ch = cur_ref.shape
    n_strip, n_tap = w_ref.shape[:2]
    n_chunk = ts // CONV_ROWS
    first_tap = CONV_HALO - (n_tap - 1)
    halo = halo_ref[...].astype(F32)
    halo = jnp.where(pl.program_id(1) > 0, halo, jnp.zeros_like(halo))
    cur = cur_ref[...].astype(F32)
    shifted_rows = ts + CONV_HALO - SUBLANES
    for c in range(n_strip):
        cs = slice(c * LANES, (c + 1) * LANES)
        xs_ref[0, c, :CONV_HALO, :] = halo[:, cs]
        xs_ref[0, c, CONV_HALO:, :] = cur[:, cs]
        for s in range(1, SUBLANES):
            xs_ref[s, c, :shifted_rows, :] = xs_ref[0, c, s:s + shifted_rows, :]

    def strip_chunk(idx, carry):
        r = idx // n_strip
        c = idx % n_strip
        row0 = pl.multiple_of(r * CONV_ROWS, CONV_ROWS)
        acc = jnp.broadcast_to(b_ref[c], (CONV_ROWS, LANES))
        for s in range(SUBLANES):
            taps = [k for k in range(n_tap) if (first_tap + k) % SUBLANES == s]
            lo = (first_tap + taps[0]) // SUBLANES * SUBLANES
            hi = (first_tap + taps[-1]) // SUBLANES * SUBLANES
            data = xs_ref[s, c, pl.ds(row0 + lo, hi - lo + CONV_ROWS), :]
            for k in taps:
                a = (first_tap + k) // SUBLANES * SUBLANES - lo
                acc = acc + w_ref[c, k:k + 1, :] * data[a:a + CONV_ROWS, :]
        acc_ref[c, pl.ds(row0, CONV_ROWS), :] = acc
        return carry

    lax.fori_loop(0, n_chunk * n_strip, strip_chunk, 0, unroll=2)
    y = jnp.concatenate([acc_ref[c] for c in range(n_strip)], axis=1)
    mu = jnp.mean(y, axis=-1, keepdims=True)
    cen = y - mu
    var = jnp.mean(cen * cen, axis=-1, keepdims=True)
    v = cen * lax.rsqrt(var + LN_EPS) * lg_ref[...] + lb_ref[...]
    o_ref[...] = _silu(v).astype(BF16)


def _dwconv(u, w, b, lg, lb, bsz, seq):
    ts = CONV_TILE
    nt = seq // ts
    ch = u.shape[1]
    hpt = ts // CONV_HALO
    n_strip = ch // LANES
    full = lambda a: pl.BlockSpec(a.shape, lambda bb, i: (0,) * a.ndim)
    w = w.reshape(-1, n_strip, LANES).transpose(1, 0, 2)
    b = b.reshape(n_strip, 1, LANES)
    return pl.pallas_call(
        _dwconv_kernel,
        grid=(bsz, nt),
        in_specs=[pl.BlockSpec((ts, ch), lambda bb, i: (bb * nt + i, 0)),
                  pl.BlockSpec((CONV_HALO, ch), lambda bb, i: (jnp.maximum((bb * nt + i) * hpt - 1, 0), 0)),
                  full(w), full(b), full(lg), full(lb)],
        out_specs=pl.BlockSpec((ts, ch), lambda bb, i: (bb * nt + i, 0)),
        out_shape=jax.ShapeDtypeStruct(u.shape, BF16),
        scratch_shapes=[pltpu.VMEM((SUBLANES, n_strip, CONV_HALO + ts, LANES), F32),
                        pltpu.VMEM((n_strip, ts, LANES), F32)],
        compiler_params=_cparams("parallel", "parallel"),
        name="dwconv_ln",
    )(u, u, w, b, lg, lb)


def _row(v):
    return v.reshape(1, -1).astype(F32)


def _pad_lanes(v, n=LANES):
    v = v.reshape(1, -1).astype(F32)
    return jnp.pad(v, ((0, 0), (0, n - v.shape[1])))


def _router_params(w_group, b_group, w_expert, b_expert):
    wr = jnp.concatenate([w_group, w_expert], axis=1)
    wr = jnp.pad(wr, ((0, 0), (0, LANES - wr.shape[1]))).astype(BF16)
    br = _pad_lanes(jnp.concatenate([b_group.reshape(-1), b_expert.reshape(-1)]))
    return wr, br


def kernel(x, mix_norm_g, w_in, q_norm_g, k_norm_g, attn_sinks, ssm_conv_w, ssm_conv_b, ssm_dt_bias, ssm_a_log, ssm_d, ssm_out_norm_g, w_out, conv_norm_g, conv_w_pw1, conv_b_pw1, conv_w_dw, conv_b_dw, conv_ln_g, conv_ln_b, conv_w_pw2, conv_b_pw2, moe_norm_g, moe_w_group, moe_b_group, moe_w_expert, moe_b_expert, moe_w_gate_up, moe_w_down):
    bsz, seq, d = x.shape
    t = bsz * seq
    n_heads = attn_sinks.shape[1]
    n_kv = n_heads // GQA_REP
    q_cols = n_heads * HEAD_DIM
    kv_cols = n_kv * HEAD_DIM
    d_inner = ssm_out_norm_g.shape[1]
    ssm_heads = ssm_a_log.shape[1]
    n_groups = moe_w_group.shape[2]
    n_experts = moe_w_expert.shape[2]
    assert t % ROW_TILE == 0 and seq % CONV_TILE == 0 and seq % (SSM_CHUNK * SSD_STEP_CHUNKS) == 0 and seq % (ATTN_BLOCK * ATTN_STEP_BLOCKS) == 0
    assert ssm_heads <= LANES and n_groups + n_experts <= LANES

    x2 = x.reshape(t, d)
    tri_strict = jnp.asarray(np.tril(np.ones((ROW_TILE, ROW_TILE), np.float32), -1), BF16)
    lane_before = jnp.asarray(np.triu(np.ones((LANES, LANES), np.float32), 1), BF16)
    zero_bias = jnp.zeros((1, d), F32)

    w = w_in[0]
    cuts = [0] + np.cumsum([q_cols, kv_cols, kv_cols, d_inner, ssm_conv_w.shape[2]]).tolist() + [w.shape[1]]
    wq, wk, wv, wz, wx, wdt = (w[:, lo:hi] for lo, hi in zip(cuts[:-1], cuts[1:]))
    wdt = jnp.pad(wdt, ((0, 0), (0, LANES - ssm_heads)))
    seg_ones = jnp.asarray(np.kron(np.eye(q_cols // HEAD_DIM, dtype=np.float32),
                                   np.ones((HEAD_DIM, HEAD_DIM), np.float32)), BF16)
    qg = jnp.tile(_row(q_norm_g[0]), (1, n_heads))
    kg = jnp.tile(_row(k_norm_g[0]), (1, n_kv))
    q, k, v, z, xbc, dt = _inproj(
        x2, _row(mix_norm_g[0]), wq.astype(BF16), wk.astype(BF16), wv.astype(BF16), wz.astype(BF16),
        wx.astype(BF16), wdt.astype(BF16), qg, kg, seg_ones)
    kv_rep = jnp.asarray(np.stack([np.kron(np.eye(n_kv, dtype=np.float32)[:, g:g + 1],
                                           np.tile(np.eye(HEAD_DIM, dtype=np.float32), (1, GQA_REP)))
                                   for g in range(n_kv)]), BF16)
    y_attn = _attention(attn_sinks[0].reshape(1, n_heads).astype(F32), kv_rep, q, k, v, bsz, seq)
    expand = jnp.asarray(np.kron(np.eye(LANES, ssm_heads, dtype=np.float32),
                                 np.ones((1, SSM_HEAD_DIM), np.float32)), BF16)
    tri_incl = jnp.asarray(np.tril(np.ones((SSM_CHUNK, SSM_CHUNK), np.float32)), BF16)
    n_tap = ssm_conv_w.shape[1]
    shift = jnp.asarray(np.concatenate([np.eye(SSM_CHUNK, 2 * SSM_CHUNK, SSM_CHUNK - j, dtype=np.float32)
                                        for j in range(1, n_tap)], axis=0), BF16)
    y_ssm = _ssd(xbc, z, dt, ssm_conv_w[0].astype(F32), _row(ssm_conv_b[0]), _pad_lanes(ssm_dt_bias[0]),
                 _pad_lanes(ssm_a_log[0]), jnp.repeat(_row(ssm_d[0]), SSM_HEAD_DIM, axis=1), _row(ssm_out_norm_g[0]),
                 expand, tri_incl, shift, bsz, seq)
    wo = w_out[0].astype(BF16)
    wr, br = _router_params(moe_w_group[0], moe_b_group[0], moe_w_expert[0], moe_b_expert[0])
    xnew, xs, gates, counts = _proj_router(
        x2, [y_attn, y_ssm], [wo[:q_cols], wo[q_cols:]], zero_bias, _row(moe_norm_g[0]), wr, br, tri_strict,
        lane_before, n_groups, n_experts)
    x2 = _moe(xnew, xs, gates, counts, moe_w_gate_up, moe_w_down, 0, n_groups)

    u = _pw1(x2, _row(conv_norm_g[0]), conv_w_pw1[0].astype(BF16), _row(conv_b_pw1[0]))
    u = _dwconv(u, conv_w_dw[0].astype(F32), _row(conv_b_dw[0]), _row(conv_ln_g[0]), _row(conv_ln_b[0]), bsz, seq)
    wr, br = _router_params(moe_w_group[1], moe_b_group[1], moe_w_expert[1], moe_b_expert[1])
    xnew, xs, gates, counts = _proj_router(
        x2, [u], [conv_w_pw2[0].astype(BF16)], _row(conv_b_pw2[0]), _row(moe_norm_g[1]), wr, br, tri_strict,
        lane_before, n_groups, n_experts)
    x2 = _moe(xnew, xs, gates, counts, moe_w_gate_up, moe_w_down, 1, n_groups)
    return x2.reshape(bsz, seq, d)
```

```python
import functools

import numpy as np
import jax
import jax.numpy as jnp
from jax import lax
from jax.experimental import pallas as pl
from jax.experimental.pallas import tpu as pltpu

F32 = jnp.float32
BF16 = jnp.bfloat16
I32 = jnp.int32
U32 = jnp.uint32

HEAD_DIM = 64
GQA_REP = 4
ATTN_BLOCK = 128
SSM_HEAD_DIM = 64
SSM_GROUPS = 2
SSM_D_STATE = 128
SSM_CHUNK = 128
MOE_TOP_K = 2
RMS_EPS = 1e-6
LN_EPS = 1e-5

LANES = 128
SUBLANES = 8

ROW_TILE = 512
EXPERT_ROWS = 512
SSD_STEP_CHUNKS = 2
ATTN_STEP_BLOCKS = 2
CONV_TILE = 512
CONV_HALO = 32
CONV_ROWS = 64
RUN_ROWS = SUBLANES
VMEM_LIMIT = 48 * 1024 * 1024


def _cparams(*sem):
    return pltpu.CompilerParams(dimension_semantics=sem, vmem_limit_bytes=VMEM_LIMIT)


def _dot(a, b):
    return jnp.dot(a, b, preferred_element_type=F32)


def _rms(x, g):
    return x * lax.rsqrt(jnp.mean(x * x, axis=-1, keepdims=True) + RMS_EPS) * g


def _silu(x):
    return x * jax.nn.sigmoid(x)


def _pack_halves(y, is_bf16=False):
    c = y.shape[1] // 2
    if is_bf16:
        return lax.bitcast_convert_type(y[:, :c], U32) | (lax.bitcast_convert_type(y[:, c:], U32) >> 16)
    hi = lax.bitcast_convert_type(y[:, :c].astype(BF16).astype(F32), U32)
    lo = lax.bitcast_convert_type(y[:, c:].astype(BF16).astype(F32), U32)
    return (hi & jnp.uint32(0xFFFF0000)) | (lo >> 16)


def _unpack_halves(u):
    hi = lax.bitcast_convert_type(u & jnp.uint32(0xFFFF0000), F32)
    lo = lax.bitcast_convert_type(u << 16, F32)
    return jnp.concatenate([hi, lo], axis=1)


def _inproj_kernel(x_ref, g_ref, wq_ref, wk_ref, wv_ref, wz_ref, wx_ref, wdt_ref, qg_ref, kg_ref, bd_ref,
                   q_out, k_out, v_out, z_out, xbc_out, dt_out):
    xn = _rms(x_ref[...], g_ref[...]).astype(BF16)

    def head_rms(y, gain):
        bd = bd_ref[:y.shape[1], :y.shape[1]]
        sq = y * y
        hi = sq.astype(BF16)
        lo = (sq - hi.astype(F32)).astype(BF16)
        ss = _dot(hi, bd) + _dot(lo, bd)
        return y * lax.rsqrt(ss * (1.0 / HEAD_DIM) + RMS_EPS) * gain

    q = head_rms(_dot(xn, wq_ref[...]), qg_ref[...])
    q_out[...] = (q * (HEAD_DIM ** -0.5)).astype(BF16)
    k_out[...] = head_rms(_dot(xn, wk_ref[...]), kg_ref[...]).astype(BF16)
    v_out[...] = _dot(xn, wv_ref[...]).astype(BF16)
    z_out[...] = _dot(xn, wz_ref[...]).astype(BF16)
    xbc_out[...] = _dot(xn, wx_ref[...]).astype(BF16)
    dt_out[...] = _dot(xn, wdt_ref[...])


def _inproj(x2, g, wq, wk, wv, wz, wx, wdt, qg, kg, bd):
    t, d = x2.shape
    tm = ROW_TILE
    row = lambda n: pl.BlockSpec((tm, n), lambda i: (i, 0))
    full = lambda a: pl.BlockSpec(a.shape, lambda i: (0,) * a.ndim)
    outs = [(wq.shape[1], BF16), (wk.shape[1], BF16), (wv.shape[1], BF16), (wz.shape[1], BF16),
            (wx.shape[1], BF16), (wdt.shape[1], F32)]
    return pl.pallas_call(
        _inproj_kernel,
        grid=(t // tm,),
        in_specs=[row(d)] + [full(a) for a in (g, wq, wk, wv, wz, wx, wdt, qg, kg, bd)],
        out_specs=[row(n) for n, _ in outs],
        out_shape=[jax.ShapeDtypeStruct((t, n), dt) for n, dt in outs],
        compiler_params=_cparams("parallel"),
        name="inproj",
    )(x2, g, wq, wk, wv, wz, wx, wdt, qg, kg, bd)


def _attn_kernel(sink_ref, rep_ref, q_ref, kc_ref, kp_ref, vc_ref, vp_ref, o_ref):
    n = pl.program_id(1)
    blk = ATTN_BLOCK
    gw = GQA_REP * HEAD_DIM
    n_kv = q_ref.shape[1] // gw
    qi = lax.broadcasted_iota(I32, (blk, 2 * blk), 0)
    kj = lax.broadcasted_iota(I32, (blk, 2 * blk), 1)
    rel = qi + blk - kj
    in_window = (rel >= 0) & (rel < ATTN_BLOCK)
    first_key = jnp.where(n > 0, 0, blk)
    key_head = lax.broadcasted_iota(I32, (2 * blk, gw), 1) // HEAD_DIM
    out_head = lax.broadcasted_iota(I32, (blk, gw), 1) // HEAD_DIM
    for sb, g in [(sb, g) for sb in range(ATTN_STEP_BLOCKS) for g in range(n_kv)]:
        sl = slice(g * gw, (g + 1) * gw)
        rows = slice(sb * blk, (sb + 1) * blk)
        before = slice((sb - 1) * blk, sb * blk)
        band = in_window & (kj >= first_key) if sb == 0 else in_window
        qg = q_ref[rows, sl]
        kcat = jnp.concatenate([kp_ref[...] if sb == 0 else kc_ref[before, :], kc_ref[rows, :]], axis=0)
        vcat = jnp.concatenate([vp_ref[...] if sb == 0 else vc_ref[before, :], vc_ref[rows, :]], axis=0)
        kk = _dot(kcat, rep_ref[g]).astype(BF16)
        vv = _dot(vcat, rep_ref[g]).astype(BF16)
        zero = jnp.zeros_like(kk)
        kbd = jnp.concatenate([jnp.where(key_head == h, kk, zero) for h in range(GQA_REP)], axis=0)
        vbd = jnp.concatenate([jnp.where(key_head == h, vv, zero) for h in range(GQA_REP)], axis=0)
        s = lax.dot_general(qg, kbd, (((1,), (1,)), ((), ())), preferred_element_type=F32)
        probs, scale = [], None
        for h in range(GQA_REP):
            sh = jnp.where(band, s[:, h * 2 * blk:(h + 1) * 2 * blk], -jnp.inf)
            sink = sink_ref[0, g * GQA_REP + h]
            m = jnp.maximum(jnp.max(sh, axis=-1, keepdims=True), sink)
            p = jnp.exp(sh - m)
            denom = jnp.sum(p, axis=-1, keepdims=True) + jnp.exp(sink - m)
            probs.append(p.astype(BF16))
            inv = jnp.broadcast_to(1.0 / denom, (blk, gw))
            scale = inv if scale is None else jnp.where(out_head == h, inv, scale)
        o = _dot(jnp.concatenate(probs, axis=1), vbd)
        o_ref[rows, sl] = (o * scale).astype(BF16)


def _attention(sinks, kv_rep, q, k, v, bsz, seq):
    blk = ATTN_BLOCK
    spb = ATTN_STEP_BLOCKS
    nb = seq // (blk * spb)
    cur = lambda w: pl.BlockSpec((blk * spb, w), lambda b, n: (b * nb + n, 0))
    prev = lambda w: pl.BlockSpec((blk, w), lambda b, n: (jnp.maximum((b * nb + n) * spb - 1, 0), 0))
    qw, kw = q.shape[1], k.shape[1]
    return pl.pallas_call(
        _attn_kernel,
        grid=(bsz, nb),
        in_specs=[pl.BlockSpec(memory_space=pltpu.SMEM), pl.BlockSpec(kv_rep.shape, lambda b, n: (0, 0, 0)),
                  cur(qw), cur(kw), prev(kw), cur(kw), prev(kw)],
        out_specs=cur(qw),
        out_shape=jax.ShapeDtypeStruct(q.shape, BF16),
        compiler_params=_cparams("parallel", "parallel"),
        name="swa_attention",
    )(sinks, kv_rep, q, k, k, v, v)


def _split3(a):
    a1 = a.astype(BF16)
    r = a - a1.astype(F32)
    a2 = r.astype(BF16)
    return a1, a2, (r - a2.astype(F32)).astype(BF16)


def _ssd_kernel(xbc_ref, z_ref, dt_ref, cw_ref, cb_ref, dtb_ref, alog_ref, dskip_ref, og_ref, expand_ref, tri_ref,
                shift_ref, y_ref, prev_ref, state_ref):
    lc = SSM_CHUNK

    @pl.when(pl.program_id(1) == 0)
    def _():
        prev_ref[...] = jnp.zeros_like(prev_ref)
        state_ref[...] = jnp.zeros_like(state_ref)

    for cc in range(SSD_STEP_CHUNKS):
        rows = slice(cc * lc, (cc + 1) * lc)
        prev = prev_ref[...] if cc == 0 else xbc_ref[(cc - 1) * lc:cc * lc, :]
        _ssd_chunk(xbc_ref[rows, :], prev, z_ref[rows, :], dt_ref[rows, :], cw_ref, cb_ref, dtb_ref, alog_ref,
                   dskip_ref, og_ref, expand_ref, tri_ref, shift_ref, y_ref.at[rows, :], state_ref)
    prev_ref[...] = xbc_ref[(SSD_STEP_CHUNKS - 1) * lc:, :]


def _ssd_chunk(xb, prev, z, dt, cw_ref, cb_ref, dtb_ref, alog_ref, dskip_ref, og_ref, expand_ref, tri_ref, shift_ref,
               y_ref, state_ref):
    lc = xb.shape[0]
    d_inner = z.shape[1]
    n_heads = d_inner // SSM_HEAD_DIM
    gn = SSM_GROUPS * SSM_D_STATE
    hpg = n_heads // SSM_GROUPS
    gw = hpg * SSM_HEAD_DIM
    n_tap = cw_ref.shape[0]

    sh = _dot(shift_ref[...], jnp.concatenate([prev, xb], axis=0))
    acc = cb_ref[...] + cw_ref[n_tap - 1:n_tap, :] * xb.astype(F32)
    for j in range(1, n_tap):
        acc = acc + cw_ref[n_tap - 1 - j:n_tap - j, :] * sh[(j - 1) * lc:j * lc, :]
    u = _silu(acc)
    xs = u[:, :d_inner]
    bm = u[:, d_inner:d_inner + gn].astype(BF16)
    cm = u[:, d_inner + gn:].astype(BF16)

    dtr = dt + dtb_ref[...]
    dt_c = jnp.maximum(dtr, 0.0) + jnp.log1p(jnp.exp(-jnp.abs(dtr)))
    a_c = dt_c * (-jnp.exp(alog_ref[...]))
    cum = _dot(tri_ref[...], jnp.concatenate(_split3(a_c), axis=1))
    acum_c = (cum[:, :LANES] + cum[:, LANES:2 * LANES]) + cum[:, 2 * LANES:]
    acum_r = acum_c.T
    ex = _dot(jnp.concatenate(_split3(dt_c) + _split3(acum_c), axis=0), expand_ref[...])
    dt_e = (ex[:lc] + ex[lc:2 * lc]) + ex[2 * lc:3 * lc]
    acum_e = (ex[3 * lc:4 * lc] + ex[4 * lc:5 * lc]) + ex[5 * lc:]
    xdt = xs * dt_e

    li = lax.broadcasted_iota(I32, (lc, lc), 0)
    si = lax.broadcasted_iota(I32, (lc, lc), 1)
    causal = li >= si
    mats = []
    for g in range(SSM_GROUPS):
        cb = lax.dot_general(cm[:, g * SSM_D_STATE:(g + 1) * SSM_D_STATE], bm[:, g * SSM_D_STATE:(g + 1) * SSM_D_STATE],
                             (((1,), (1,)), ((), ())), preferred_element_type=F32)
        for r in range(hpg):
            h = g * hpg + r
            seg = acum_c[:, h:h + 1] - acum_r[h:h + 1, :]
            decay = jnp.exp(jnp.where(causal, seg, -jnp.inf))
            mats.append((cb * decay).astype(BF16))
    xdt_b = xdt.astype(BF16)
    row_head = lax.broadcasted_iota(I32, (lc, d_inner), 1) // SSM_HEAD_DIM
    zero = jnp.zeros_like(xdt_b)
    xbd = jnp.concatenate([jnp.where(row_head == h, xdt_b, zero) for h in range(n_heads)], axis=0)
    y = _dot(jnp.concatenate(mats, axis=1), xbd)

    a_last = acum_e[lc - 1:lc, :]
    w_state = (xdt * jnp.exp(a_last - acum_e)).astype(BF16)
    y_off = []
    for g in range(SSM_GROUPS):
        ns = slice(g * SSM_D_STATE, (g + 1) * SSM_D_STATE)
        hs = slice(g * gw, (g + 1) * gw)
        h_in = state_ref[:, hs]
        y_off.append(_dot(cm[:, ns], h_in.astype(BF16)))
        new = lax.dot_general(bm[:, ns], w_state[:, hs], (((0,), (0,)), ((), ())), preferred_element_type=F32)
        state_ref[:, hs] = h_in * jnp.exp(a_last[:, hs]) + new
    y = y + jnp.concatenate(y_off, axis=1) * jnp.exp(acum_e)

    y = y + dskip_ref[...] * xs
    y = y * _silu(z.astype(F32))
    y_ref[...] = _rms(y, og_ref[...]).astype(BF16)


def _ssd(xbc, z, dt, cw, cb, dtb, alog, dskip, og, expand, tri, shift, bsz, seq):
    rows = SSM_CHUNK * SSD_STEP_CHUNKS
    ns = seq // rows
    blk = lambda n: pl.BlockSpec((rows, n), lambda b, c: (b * ns + c, 0))
    full = lambda a: pl.BlockSpec(a.shape, lambda b, c: (0,) * a.ndim)
    d_inner = z.shape[1]
    return pl.pallas_call(
        _ssd_kernel,
        grid=(bsz, ns),
        in_specs=[blk(xbc.shape[1]), blk(d_inner), blk(dt.shape[1])]
        + [full(a) for a in (cw, cb, dtb, alog, dskip, og, expand, tri, shift)],
        out_specs=blk(d_inner),
        out_shape=jax.ShapeDtypeStruct(z.shape, BF16),
        scratch_shapes=[pltpu.VMEM((SSM_CHUNK, xbc.shape[1]), BF16), pltpu.VMEM((SSM_D_STATE, d_inner), F32)],
        compiler_params=_cparams("arbitrary", "arbitrary"),
        name="ssd_scan",
    )(xbc, z, dt, cw, cb, dtb, alog, dskip, og, expand, tri, shift)


def _sorted_rows(tm, n_experts):
    return MOE_TOP_K * tm + n_experts * RUN_ROWS


def _proj_router_kernel(n_act, n_groups, n_experts, *refs):
    x_ref = refs[0]
    act_refs = refs[1:1 + n_act]
    w_refs = refs[1 + n_act:1 + 2 * n_act]
    b_ref, g_ref, wr_ref, br_ref, tri_ref, ltri_ref = refs[1 + 2 * n_act:7 + 2 * n_act]
    xnew_ref, xs_ref, gates_ref, counts_ref = refs[7 + 2 * n_act:]
    tm = x_ref.shape[0]
    epg = n_experts // n_groups

    y = x_ref[...] + b_ref[...]
    for a_ref, w_ref in zip(act_refs, w_refs):
        y = y + _dot(a_ref[...], w_ref[...])
    xnew_ref[...] = y
    xn = _rms(y, g_ref[...]).astype(BF16)

    logits = _dot(xn, wr_ref[...]) + br_ref[...]
    lane = lax.broadcasted_iota(I32, logits.shape, 1)
    big = jnp.int32(LANES)
    neg = -jnp.inf

    def first_argmax(v):
        m = jnp.max(v, axis=-1, keepdims=True)
        return m, jnp.min(jnp.where(v == m, lane, big), axis=-1, keepdims=True)

    gl = jnp.where(lane < n_groups, logits, neg)
    gmax, gsel = first_argmax(gl)
    g_w = 1.0 / jnp.sum(jnp.exp(gl - gmax), axis=-1, keepdims=True)
    elane = lane - n_groups
    in_group = (elane >= 0) & (elane < n_experts) & ((elane // epg) == gsel)
    el = jnp.where(in_group, logits, neg)
    top1, i1 = first_argmax(el)
    top2, i2 = first_argmax(jnp.where(lane == i1, neg, el))
    e2 = jnp.exp(top2 - top1)
    w1 = 1.0 / (1.0 + e2)
    gate1 = g_w * w1
    gate2 = g_w * (e2 * w1)

    sel1 = lane == i1
    sel2 = lane == i2
    onehot = jnp.where(sel1, 1.0, jnp.where(sel2, 1.0, 0.0))
    before = _dot(tri_ref[...], onehot.astype(BF16))
    cnt = jnp.sum(onehot, axis=0, keepdims=True)
    counts_ref[...] = cnt
    cnt_pad = jnp.floor((cnt + (RUN_ROWS - 1)) * (1.0 / RUN_ROWS)) * RUN_ROWS
    run_start = _dot(jnp.broadcast_to(cnt_pad, (SUBLANES, LANES)).astype(BF16), ltri_ref[...])[0:1, :]
    slot = run_start + before
    j1 = jnp.sum(jnp.where(sel1, slot, 0.0), axis=-1, keepdims=True)
    j2 = jnp.sum(jnp.where(sel2, slot, 0.0), axis=-1, keepdims=True)
    table = jnp.where(lane == 0, gate1, jnp.where(lane == 1, gate2,
                                                  jnp.where(lane == 2, j1, jnp.where(lane == 3, j2, 0.0))))
    gates_ref[...] = table

    tt = table.T
    j1r = tt[2:3, :].astype(I32)
    j2r = tt[3:4, :].astype(I32)
    ri = lax.broadcasted_iota(I32, (xs_ref.shape[0], tm), 0)
    perm = jnp.where(ri == j1r, 1.0, jnp.where(ri == j2r, 1.0, 0.0)).astype(BF16)
    xs_ref[...] = _pack_halves(_dot(perm, xn), is_bf16=True)


def _proj_router(x2, acts, ws, bias, g, wr, br, tri, ltri, n_groups, n_experts):
    t, d = x2.shape
    tm = ROW_TILE
    n_tiles = t // tm
    lr = _sorted_rows(tm, n_experts)
    row = lambda n: pl.BlockSpec((tm, n), lambda i: (i, 0))
    full = lambda a: pl.BlockSpec(a.shape, lambda i: (0,) * a.ndim)
    kern = functools.partial(_proj_router_kernel, len(acts), n_groups, n_experts)
    return pl.pallas_call(
        kern,
        grid=(n_tiles,),
        in_specs=[row(d)] + [row(a.shape[1]) for a in acts] + [full(a) for a in (*ws, bias, g, wr, br, tri, ltri)],
        out_specs=[row(d), pl.BlockSpec((lr, d // 2), lambda i: (i, 0)), row(LANES),
                   pl.BlockSpec((None, 1, LANES), lambda i: (i, 0, 0))],
        out_shape=[jax.ShapeDtypeStruct((t, d), F32), jax.ShapeDtypeStruct((n_tiles * lr, d // 2), U32),
                   jax.ShapeDtypeStruct((t, LANES), F32), jax.ShapeDtypeStruct((n_tiles, 1, LANES), F32)],
        compiler_params=_cparams("parallel"),
        name="proj_router",
    )(x2, *acts, *ws, bias, g, wr, br, tri, ltri)


def _group_copies(table_ref, base, n, src_ref, dst_ref, sem):
    for q in range(n):
        g = table_ref[base + q]
        pltpu.make_async_copy(src_ref.at[pl.ds(pl.multiple_of(g * RUN_ROWS, RUN_ROWS), RUN_ROWS)],
                              dst_ref.at[pl.ds(q * RUN_ROWS, RUN_ROWS)], sem).start(priority=q % 2)


def _experts_kernel(be_ref, nused_ref, src_ref, xs_ref, wgu_ref, wd_ref, y_ref, xbuf, sems, wgu_b, wd_b):
    i = pl.program_id(0)
    n_used = nused_ref[0]
    rows = y_ref.shape[0]
    gpb = rows // RUN_ROWS
    slot = i % 2

    def fetch(blk, s):
        _group_copies(src_ref, blk * gpb, gpb, xs_ref, xbuf.at[s], sems.at[s])

    def wait(s):
        pltpu.make_async_copy(xs_ref.at[pl.ds(0, rows)], xbuf.at[s], sems.at[s]).wait()

    @pl.when(i == 0)
    def _():
        fetch(0, 0)

    @pl.when((i == 0) | (be_ref[i] != be_ref[jnp.maximum(i - 1, 0)]))
    def _():
        wgu_b[...] = wgu_ref[...].astype(BF16)
        wd_b[...] = wd_ref[...].astype(BF16)

    @pl.when(i >= n_used)
    def _():
        y_ref[...] = jnp.zeros_like(y_ref)

    @pl.when(i == n_used)
    def _():
        wait(slot)

    @pl.when(i < n_used)
    def _():
        wait(slot)
        fetch(i + 1, 1 - slot)
        ff = wd_b.shape[0]
        x = _unpack_halves(xbuf[slot]).astype(BF16)
        gu = _dot(x, wgu_b[...])
        h = (_silu(gu[:, :ff]) * gu[:, ff:]).astype(BF16)
        y_ref[...] = _pack_halves(_dot(h, wd_b[...]))


def _experts(block_expert, n_used, src_groups, xs, wgu, wd, layer, n_blocks):
    w = xs.shape[1]
    d, ff2 = wgu.shape[2:]
    rows = EXPERT_ROWS
    grid_spec = pltpu.PrefetchScalarGridSpec(
        num_scalar_prefetch=3,
        grid=(n_blocks,),
        in_specs=[pl.BlockSpec(memory_space=pl.ANY),
                  pl.BlockSpec((None, None, d, ff2), lambda i, be, nu, sg: (layer, be[i], 0, 0)),
                  pl.BlockSpec((None, None, ff2 // 2, d), lambda i, be, nu, sg: (layer, be[i], 0, 0))],
        out_specs=pl.BlockSpec((rows, w), lambda i, be, nu, sg: (i, 0)),
        scratch_shapes=[pltpu.VMEM((2, rows, w), U32), pltpu.SemaphoreType.DMA((2,)),
                        pltpu.VMEM((d, ff2), BF16), pltpu.VMEM((ff2 // 2, d), BF16)],
    )
    return pl.pallas_call(
        _experts_kernel,
        grid_spec=grid_spec,
        out_shape=jax.ShapeDtypeStruct((n_blocks * rows, w), U32),
        compiler_params=_cparams("arbitrary"),
        name="moe_experts",
    )(block_expert, n_used, src_groups, xs, wgu, wd)


def _combine_kernel(dst_ref, x_ref, gates_ref, y_ref, o_ref, ybuf, sems):
    i = pl.program_id(0)
    n_tiles = pl.num_programs(0)
    tm = x_ref.shape[0]
    lr = ybuf.shape[1]
    gpt = lr // RUN_ROWS
    slot = i % 2

    def fetch(tile, s):
        _group_copies(dst_ref, tile * gpt, gpt, y_ref, ybuf.at[s], sems.at[s])

    def wait(s):
        pltpu.make_async_copy(y_ref.at[pl.ds(0, lr)], ybuf.at[s], sems.at[s]).wait()

    @pl.when(i == 0)
    def _():
        fetch(0, 0)

    wait(slot)
    fetch(jnp.minimum(i + 1, n_tiles - 1), 1 - slot)
    ys = _unpack_halves(ybuf[slot]).astype(BF16)
    table = gates_ref[...]
    col = lax.broadcasted_iota(I32, (tm, lr), 1)
    pick = jnp.zeros((tm, lr), F32)
    for k in range(MOE_TOP_K):
        jk = table[:, MOE_TOP_K + k:MOE_TOP_K + k + 1].astype(I32)
        pick = jnp.where(col == jk, table[:, k:k + 1], pick)
    o_ref[...] = x_ref[...] + _dot(pick.astype(BF16), ys)

    @pl.when(i == n_tiles - 1)
    def _():
        wait(1 - slot)


def _combine(dst_groups, xnew, gates, y, lr):
    t, d = xnew.shape
    tm = ROW_TILE
    grid_spec = pltpu.PrefetchScalarGridSpec(
        num_scalar_prefetch=1,
        grid=(t // tm,),
        in_specs=[pl.BlockSpec((tm, d), lambda i, p: (i, 0)),
                  pl.BlockSpec((tm, LANES), lambda i, p: (i, 0)),
                  pl.BlockSpec(memory_space=pl.ANY)],
        out_specs=pl.BlockSpec((tm, d), lambda i, p: (i, 0)),
        scratch_shapes=[pltpu.VMEM((2, lr, y.shape[1]), U32), pltpu.SemaphoreType.DMA((2,))],
    )
    return pl.pallas_call(
        _combine_kernel,
        grid_spec=grid_spec,
        out_shape=jax.ShapeDtypeStruct((t, d), F32),
        compiler_params=_cparams("arbitrary"),
        name="moe_combine",
    )(dst_groups, xnew, gates, y)


def _moe_plan(counts, n_groups, n_experts, lr):
    n_tiles = counts.shape[0]
    gpb = EXPERT_ROWS // RUN_ROWS
    gpt = lr // RUN_ROWS
    cnt = counts[:, 0, n_groups:n_groups + n_experts].astype(I32)
    run = (cnt + RUN_ROWS - 1) // RUN_ROWS
    run_end = jnp.cumsum(run, axis=1)
    run_off = run_end - run
    used = run_end[:, -1]
    seg = jnp.sum(run, axis=0)
    seg_pad = (seg + gpb - 1) // gpb * gpb
    seg_end = jnp.cumsum(seg_pad)
    seg_start = seg_end - seg_pad
    before = jnp.cumsum(run, axis=0) - run
    n_blocks = (n_tiles * gpt + n_experts * (gpb - 1)) // gpb + 1
    n_used = (seg_end[-1] // gpb).astype(I32)
    blk = jnp.arange(n_blocks, dtype=I32)
    first = jnp.minimum(blk, n_used - 1) * gpb
    block_expert = jnp.minimum(jnp.sum(seg_end[None, :] <= first[:, None], axis=1), n_experts - 1).astype(I32)

    zero_dst = n_blocks * gpb - 1
    q = jnp.arange(gpt, dtype=I32)
    owner = jnp.sum(run_end[:, None, :] <= q[None, :, None], axis=2)
    own = jax.nn.one_hot(jnp.minimum(owner, n_experts - 1), n_experts, dtype=I32)
    shift = seg_start[None, :] + before - run_off
    dst = q[None, :] + jnp.sum(own * shift[:, None, :], axis=2)
    dst = jnp.where(q[None, :] < used[:, None], dst, zero_dst).astype(I32).reshape(-1)

    pick = lambda onehot, tab: jnp.dot(onehot, tab.astype(F32), precision=lax.Precision.HIGHEST)
    zero_src = gpt - 1
    g = jnp.arange(n_blocks * gpb, dtype=I32)
    ge = jnp.minimum(jnp.sum(seg_end[None, :] <= g[:, None], axis=1), n_experts - 1)
    oh_e = (ge[:, None] == jnp.arange(n_experts, dtype=I32)[None, :]).astype(F32)
    per_e = pick(oh_e, jnp.stack([seg_start, seg], axis=1))
    m = g.astype(F32) - per_e[:, 0]
    tile_end = pick(oh_e, jnp.cumsum(run, axis=0).T)
    tile = jnp.minimum(jnp.sum(tile_end <= m[:, None], axis=1), n_tiles - 1)
    oh_t = (tile[:, None] == jnp.arange(n_tiles, dtype=I32)[None, :]).astype(F32)
    tile_base = (jnp.arange(n_tiles, dtype=I32)[:, None] * gpt + run_off - before).T
    src = jnp.sum(oh_t * pick(oh_e, tile_base), axis=1) + m
    src = jnp.where(m < per_e[:, 1], src, float(zero_src)).astype(I32)
    return block_expert, n_used.reshape(1), src, dst, n_blocks


def _moe(xnew, xs, gates, counts, wgu, wd, layer, n_groups):
    n_experts = wgu.shape[1]
    lr = _sorted_rows(ROW_TILE, n_experts)
    block_expert, n_used, src, dst, n_blocks = _moe_plan(counts, n_groups, n_experts, lr)
    y = _experts(block_expert, n_used, src, xs, wgu, wd, layer, n_blocks)
    return _combine(dst, xnew, gates, y, lr)


def _pw1_kernel(x_ref, g_ref, w_ref, b_ref, u_ref):
    ch = u_ref.shape[1]
    xn = _rms(x_ref[...], g_ref[...]).astype(BF16)
    y = _dot(xn, w_ref[...]) + b_ref[...]
    u_ref[...] = (y[:, :ch] * jax.nn.sigmoid(y[:, ch:])).astype(BF16)


def _pw1(x2, g, w, b):
    t, d = x2.shape
    tm = ROW_TILE
    ch = w.shape[1] // 2
    full = lambda a: pl.BlockSpec(a.shape, lambda i: (0,) * a.ndim)
    return pl.pallas_call(
        _pw1_kernel,
        grid=(t // tm,),
        in_specs=[pl.BlockSpec((tm, d), lambda i: (i, 0)), full(g), full(w), full(b)],
        out_specs=pl.BlockSpec((tm, ch), lambda i: (i, 0)),
        out_shape=jax.ShapeDtypeStruct((t, ch), BF16),
        compiler_params=_cparams("parallel"),
        name="pw1_glu",
    )(x2, g, w, b)


def _dwconv_kernel(cur_ref, halo_ref, w_ref, b_ref, lg_ref, lb_ref, o_ref, xs_ref, acc_ref):
    ts, ch = cur_ref.shape
    n_strip, n_tap = w_ref.shape[:2]
    n_chunk = ts // CONV_ROWS
    first_tap = CONV_HALO - (n_tap - 1)
    halo = halo_ref[...].astype(F32)
    halo = jnp.where(pl.program_id(1) > 0, halo, jnp.zeros_like(halo))
    cur = cur_ref[...].astype(F32)
    shifted_rows = ts + CONV_HALO - SUBLANES
    for c in range(n_strip):
        cs = slice(c * LANES, (c + 1) * LANES)
        xs_ref[0, c, :CONV_HALO, :] = halo[:, cs]
        xs_ref[0, c, CONV_HALO:, :] = cur[:, cs]
        for s in range(1, SUBLANES):
            xs_ref[s, c, :shifted_rows, :] = xs_ref[0, c, s:s + shifted_rows, :]

    def strip_chunk(idx, carry):
        r = idx // n_strip
        c = idx % n_strip
        row0 = pl.multiple_of(r * CONV_ROWS, CONV_ROWS)
        acc = jnp.broadcast_to(b_ref[c], (CONV_ROWS, LANES))
        for s in range(SUBLANES):
            taps = [k for k in range(n_tap) if (first_tap + k) % SUBLANES == s]
            lo = (first_tap + taps[0]) // SUBLANES * SUBLANES
            hi = (first_tap + taps[-1]) // SUBLANES * SUBLANES
            data = xs_ref[s, c, pl.ds(row0 + lo, hi - lo + CONV_ROWS), :]
            for k in taps:
                a = (first_tap + k) // SUBLANES * SUBLANES - lo
                acc = acc + w_ref[c, k:k + 1, :] * data[a:a + CONV_ROWS, :]
        acc_ref[c, pl.ds(row0, CONV_ROWS), :] = acc
        return carry

    lax.fori_loop(0, n_chunk * n_strip, strip_chunk, 0, unroll=2)
    y = jnp.concatenate([acc_ref[c] for c in range(n_strip)], axis=1)
    mu = jnp.mean(y, axis=-1, keepdims=True)
    cen = y - mu
    var = jnp.mean(cen * cen, axis=-1, keepdims=True)
    v = cen * lax.rsqrt(var + LN_EPS) * lg_ref[...] + lb_ref[...]
    o_ref[...] = _silu(v).astype(BF16)


def _dwconv(u, w, b, lg, lb, bsz, seq):
    ts = CONV_TILE
    nt = seq // ts
    ch = u.shape[1]
    hpt = ts // CONV_HALO
    n_strip = ch // LANES
    full = lambda a: pl.BlockSpec(a.shape, lambda bb, i: (0,) * a.ndim)
    w = w.reshape(-1, n_strip, LANES).transpose(1, 0, 2)
    b = b.reshape(n_strip, 1, LANES)
    return pl.pallas_call(
        _dwconv_kernel,
        grid=(bsz, nt),
        in_specs=[pl.BlockSpec((ts, ch), lambda bb, i: (bb * nt + i, 0)),
                  pl.BlockSpec((CONV_HALO, ch), lambda bb, i: (jnp.maximum((bb * nt + i) * hpt - 1, 0), 0)),
                  full(w), full(b), full(lg), full(lb)],
        out_specs=pl.BlockSpec((ts, ch), lambda bb, i: (bb * nt + i, 0)),
        out_shape=jax.ShapeDtypeStruct(u.shape, BF16),
        scratch_shapes=[pltpu.VMEM((SUBLANES, n_strip, CONV_HALO + ts, LANES), F32),
                        pltpu.VMEM((n_strip, ts, LANES), F32)],
        compiler_params=_cparams("parallel", "parallel"),
        name="dwconv_ln",
    )(u, u, w, b, lg, lb)


def _row(v):
    return v.reshape(1, -1).astype(F32)


def _pad_lanes(v, n=LANES):
    v = v.reshape(1, -1).astype(F32)
    return jnp.pad(v, ((0, 0), (0, n - v.shape[1])))


def _router_params(w_group, b_group, w_expert, b_expert):
    wr = jnp.concatenate([w_group, w_expert], axis=1)
    wr = jnp.pad(wr, ((0, 0), (0, LANES - wr.shape[1]))).astype(BF16)
    br = _pad_lanes(jnp.concatenate([b_group.reshape(-1), b_expert.reshape(-1)]))
    return wr, br


def kernel(x, mix_norm_g, w_in, q_norm_g, k_norm_g, attn_sinks, ssm_conv_w, ssm_conv_b, ssm_dt_bias, ssm_a_log, ssm_d, ssm_out_norm_g, w_out, conv_norm_g, conv_w_pw1, conv_b_pw1, conv_w_dw, conv_b_dw, conv_ln_g, conv_ln_b, conv_w_pw2, conv_b_pw2, moe_norm_g, moe_w_group, moe_b_group, moe_w_expert, moe_b_expert, moe_w_gate_up, moe_w_down):
    bsz, seq, d = x.shape
    t = bsz * seq
    n_heads = attn_sinks.shape[1]
    n_kv = n_heads // GQA_REP
    q_cols = n_heads * HEAD_DIM
    kv_cols = n_kv * HEAD_DIM
    d_inner = ssm_out_norm_g.shape[1]
    ssm_heads = ssm_a_log.shape[1]
    n_groups = moe_w_group.shape[2]
    n_experts = moe_w_expert.shape[2]
    assert t % ROW_TILE == 0 and seq % CONV_TILE == 0 and seq % (SSM_CHUNK * SSD_STEP_CHUNKS) == 0 and seq % (ATTN_BLOCK * ATTN_STEP_BLOCKS) == 0
    assert ssm_heads <= LANES and n_groups + n_experts <= LANES

    x2 = x.reshape(t, d)
    tri_strict = jnp.asarray(np.tril(np.ones((ROW_TILE, ROW_TILE), np.float32), -1), BF16)
    lane_before = jnp.asarray(np.triu(np.ones((LANES, LANES), np.float32), 1), BF16)
    zero_bias = jnp.zeros((1, d), F32)

    w = w_in[0]
    cuts = [0] + np.cumsum([q_cols, kv_cols, kv_cols, d_inner, ssm_conv_w.shape[2]]).tolist() + [w.shape[1]]
    wq, wk, wv, wz, wx, wdt = (w[:, lo:hi] for lo, hi in zip(cuts[:-1], cuts[1:]))
    wdt = jnp.pad(wdt, ((0, 0), (0, LANES - ssm_heads)))
    seg_ones = jnp.asarray(np.kron(np.eye(q_cols // HEAD_DIM, dtype=np.float32),
                                   np.ones((HEAD_DIM, HEAD_DIM), np.float32)), BF16)
    qg = jnp.tile(_row(q_norm_g[0]), (1, n_heads))
    kg = jnp.tile(_row(k_norm_g[0]), (1, n_kv))
    q, k, v, z, xbc, dt = _inproj(
        x2, _row(mix_norm_g[0]), wq.astype(BF16), wk.astype(BF16), wv.astype(BF16), wz.astype(BF16),
        wx.astype(BF16), wdt.astype(BF16), qg, kg, seg_ones)
    kv_rep = jnp.asarray(np.stack([np.kron(np.eye(n_kv, dtype=np.float32)[:, g:g + 1],
                                           np.tile(np.eye(HEAD_DIM, dtype=np.float32), (1, GQA_REP)))
                                   for g in range(n_kv)]), BF16)
    y_attn = _attention(attn_sinks[0].reshape(1, n_heads).astype(F32), kv_rep, q, k, v, bsz, seq)
    expand = jnp.asarray(np.kron(np.eye(LANES, ssm_heads, dtype=np.float32),
                                 np.ones((1, SSM_HEAD_DIM), np.float32)), BF16)
    tri_incl = jnp.asarray(np.tril(np.ones((SSM_CHUNK, SSM_CHUNK), np.float32)), BF16)
    n_tap = ssm_conv_w.shape[1]
    shift = jnp.asarray(np.concatenate([np.eye(SSM_CHUNK, 2 * SSM_CHUNK, SSM_CHUNK - j, dtype=np.float32)
                                        for j in range(1, n_tap)], axis=0), BF16)
    y_ssm = _ssd(xbc, z, dt, ssm_conv_w[0].astype(F32), _row(ssm_conv_b[0]), _pad_lanes(ssm_dt_bias[0]),
                 _pad_lanes(ssm_a_log[0]), jnp.repeat(_row(ssm_d[0]), SSM_HEAD_DIM, axis=1), _row(ssm_out_norm_g[0]),
                 expand, tri_incl, shift, bsz, seq)
    wo = w_out[0].astype(BF16)
    wr, br = _router_params(moe_w_group[0], moe_b_group[0], moe_w_expert[0], moe_b_expert[0])
    xnew, xs, gates, counts = _proj_router(
        x2, [y_attn, y_ssm], [wo[:q_cols], wo[q_cols:]], zero_bias, _row(moe_norm_g[0]), wr, br, tri_strict,
        lane_before, n_groups, n_experts)
    x2 = _moe(xnew, xs, gates, counts, moe_w_gate_up, moe_w_down, 0, n_groups)

    u = _pw1(x2, _row(conv_norm_g[0]), conv_w_pw1[0].astype(BF16), _row(conv_b_pw1[0]))
    u = _dwconv(u, conv_w_dw[0].astype(F32), _row(conv_b_dw[0]), _row(conv_ln_g[0]), _row(conv_ln_b[0]), bsz, seq)
    wr, br = _router_params(moe_w_group[1], moe_b_group[1], moe_w_expert[1], moe_b_expert[1])
    xnew, xs, gates, counts = _proj_router(
        x2, [u], [conv_w_pw2[0].astype(BF16)], _row(conv_b_pw2[0]), _row(moe_norm_g[1]), wr, br, tri_strict,
        lane_before, n_groups, n_experts)
    x2 = _moe(xnew, xs, gates, counts, moe_w_gate_up, moe_w_down, 1, n_groups)
    return x2.reshape(bsz, seq, d)
```

```python
import functools

import numpy as np
import jax
import jax.numpy as jnp
from jax import lax
from jax.experimental import pallas as pl
from jax.experimental.pallas import tpu as pltpu

F32 = jnp.float32
BF16 = jnp.bfloat16
I32 = jnp.int32
U32 = jnp.uint32

HEAD_DIM = 64
GQA_REP = 4
ATTN_BLOCK = 128
SSM_HEAD_DIM = 64
SSM_GROUPS = 2
SSM_D_STATE = 128
SSM_CHUNK = 128
MOE_TOP_K = 2
RMS_EPS = 1e-6
LN_EPS = 1e-5

LANES = 128
SUBLANES = 8

ROW_TILE = 512
EXPERT_ROWS = 512
SSD_STEP_CHUNKS = 2
ATTN_STEP_BLOCKS = 2
CONV_TILE = 512
CONV_HALO = 32
CONV_ROWS = 64
RUN_ROWS = SUBLANES
COPY_GROUPS = 4
COPY_DST_BITS = 8
VMEM_LIMIT = 48 * 1024 * 1024


def _cparams(*sem):
    return pltpu.CompilerParams(dimension_semantics=sem, vmem_limit_bytes=VMEM_LIMIT)


def _dot(a, b):
    return jnp.dot(a, b, preferred_element_type=F32)


def _rms(x, g):
    return x * lax.rsqrt(jnp.mean(x * x, axis=-1, keepdims=True) + RMS_EPS) * g


def _silu(x):
    return x * jax.nn.sigmoid(x)


def _pack_halves(y, is_bf16=False):
    c = y.shape[1] // 2
    if is_bf16:
        return lax.bitcast_convert_type(y[:, :c], U32) | (lax.bitcast_convert_type(y[:, c:], U32) >> 16)
    hi = lax.bitcast_convert_type(y[:, :c].astype(BF16).astype(F32), U32)
    lo = lax.bitcast_convert_type(y[:, c:].astype(BF16).astype(F32), U32)
    return (hi & jnp.uint32(0xFFFF0000)) | (lo >> 16)


def _unpack_halves(u):
    hi = lax.bitcast_convert_type(u & jnp.uint32(0xFFFF0000), F32)
    lo = lax.bitcast_convert_type(u << 16, F32)
    return jnp.concatenate([hi, lo], axis=1)


def _inproj_kernel(x_ref, g_ref, wq_ref, wk_ref, wv_ref, wz_ref, wx_ref, wdt_ref, qg_ref, kg_ref, bd_ref,
                   q_out, k_out, v_out, z_out, xbc_out, dt_out):
    xn = _rms(x_ref[...], g_ref[...]).astype(BF16)

    def head_rms(y, gain):
        bd = bd_ref[:y.shape[1], :y.shape[1]]
        sq = y * y
        hi = sq.astype(BF16)
        lo = (sq - hi.astype(F32)).astype(BF16)
        ss = _dot(hi, bd) + _dot(lo, bd)
        return y * lax.rsqrt(ss * (1.0 / HEAD_DIM) + RMS_EPS) * gain

    q = head_rms(_dot(xn, wq_ref[...]), qg_ref[...])
    q_out[...] = (q * (HEAD_DIM ** -0.5)).astype(BF16)
    k_out[...] = head_rms(_dot(xn, wk_ref[...]), kg_ref[...]).astype(BF16)
    v_out[...] = _dot(xn, wv_ref[...]).astype(BF16)
    z_out[...] = _dot(xn, wz_ref[...]).astype(BF16)
    xbc_out[...] = _dot(xn, wx_ref[...]).astype(BF16)
    dt_out[...] = _dot(xn, wdt_ref[...])


def _inproj(x2, g, wq, wk, wv, wz, wx, wdt, qg, kg, bd):
    t, d = x2.shape
    tm = ROW_TILE
    row = lambda n: pl.BlockSpec((tm, n), lambda i: (i, 0))
    full = lambda a: pl.BlockSpec(a.shape, lambda i: (0,) * a.ndim)
    outs = [(wq.shape[1], BF16), (wk.shape[1], BF16), (wv.shape[1], BF16), (wz.shape[1], BF16),
            (wx.shape[1], BF16), (wdt.shape[1], F32)]
    return pl.pallas_call(
        _inproj_kernel,
        grid=(t // tm,),
        in_specs=[row(d)] + [full(a) for a in (g, wq, wk, wv, wz, wx, wdt, qg, kg, bd)],
        out_specs=[row(n) for n, _ in outs],
        out_shape=[jax.ShapeDtypeStruct((t, n), dt) for n, dt in outs],
        compiler_params=_cparams("parallel"),
        name="inproj",
    )(x2, g, wq, wk, wv, wz, wx, wdt, qg, kg, bd)


def _attn_kernel(sink_ref, rep_ref, q_ref, kc_ref, kp_ref, vc_ref, vp_ref, o_ref):
    n = pl.program_id(1)
    blk = ATTN_BLOCK
    gw = GQA_REP * HEAD_DIM
    n_kv = q_ref.shape[1] // gw
    qi = lax.broadcasted_iota(I32, (blk, 2 * blk), 0)
    kj = lax.broadcasted_iota(I32, (blk, 2 * blk), 1)
    rel = qi + blk - kj
    in_window = (rel >= 0) & (rel < ATTN_BLOCK)
    first_key = jnp.where(n > 0, 0, blk)
    key_head = lax.broadcasted_iota(I32, (2 * blk, gw), 1) // HEAD_DIM
    out_head = lax.broadcasted_iota(I32, (blk, gw), 1) // HEAD_DIM
    for sb, g in [(sb, g) for sb in range(ATTN_STEP_BLOCKS) for g in range(n_kv)]:
        sl = slice(g * gw, (g + 1) * gw)
        rows = slice(sb * blk, (sb + 1) * blk)
        before = slice((sb - 1) * blk, sb * blk)
        band = in_window & (kj >= first_key) if sb == 0 else in_window
        qg = q_ref[rows, sl]
        kcat = jnp.concatenate([kp_ref[...] if sb == 0 else kc_ref[before, :], kc_ref[rows, :]], axis=0)
        vcat = jnp.concatenate([vp_ref[...] if sb == 0 else vc_ref[before, :], vc_ref[rows, :]], axis=0)
        kk = _dot(kcat, rep_ref[g]).astype(BF16)
        vv = _dot(vcat, rep_ref[g]).astype(BF16)
        zero = jnp.zeros_like(kk)
        kbd = jnp.concatenate([jnp.where(key_head == h, kk, zero) for h in range(GQA_REP)], axis=0)
        vbd = jnp.concatenate([jnp.where(key_head == h, vv, zero) for h in range(GQA_REP)], axis=0)
        s = lax.dot_general(qg, kbd, (((1,), (1,)), ((), ())), preferred_element_type=F32)
        probs, scale = [], None
        for h in range(GQA_REP):
            sh = jnp.where(band, s[:, h * 2 * blk:(h + 1) * 2 * blk], -jnp.inf)
            sink = sink_ref[0, g * GQA_REP + h]
            m = jnp.maximum(jnp.max(sh, axis=-1, keepdims=True), sink)
            p = jnp.exp(sh - m)
            denom = jnp.sum(p, axis=-1, keepdims=True) + jnp.exp(sink - m)
            probs.append(p.astype(BF16))
            inv = jnp.broadcast_to(1.0 / denom, (blk, gw))
            scale = inv if scale is None else jnp.where(out_head == h, inv, scale)
        o = _dot(jnp.concatenate(probs, axis=1), vbd)
        o_ref[rows, sl] = (o * scale).astype(BF16)


def _attention(sinks, kv_rep, q, k, v, bsz, seq):
    blk = ATTN_BLOCK
    spb = ATTN_STEP_BLOCKS
    nb = seq // (blk * spb)
    cur = lambda w: pl.BlockSpec((blk * spb, w), lambda b, n: (b * nb + n, 0))
    prev = lambda w: pl.BlockSpec((blk, w), lambda b, n: (jnp.maximum((b * nb + n) * spb - 1, 0), 0))
    qw, kw = q.shape[1], k.shape[1]
    return pl.pallas_call(
        _attn_kernel,
        grid=(bsz, nb),
        in_specs=[pl.BlockSpec(memory_space=pltpu.SMEM), pl.BlockSpec(kv_rep.shape, lambda b, n: (0, 0, 0)),
                  cur(qw), cur(kw), prev(kw), cur(kw), prev(kw)],
        out_specs=cur(qw),
        out_shape=jax.ShapeDtypeStruct(q.shape, BF16),
        compiler_params=_cparams("parallel", "parallel"),
        name="swa_attention",
    )(sinks, kv_rep, q, k, k, v, v)


def _split3(a):
    a1 = a.astype(BF16)
    r = a - a1.astype(F32)
    a2 = r.astype(BF16)
    return a1, a2, (r - a2.astype(F32)).astype(BF16)


def _ssd_kernel(xbc_ref, z_ref, dt_ref, cw_ref, cb_ref, dtb_ref, alog_ref, dskip_ref, og_ref, expand_ref, tri_ref,
                shift_ref, y_ref, prev_ref, state_ref):
    lc = SSM_CHUNK

    @pl.when(pl.program_id(1) == 0)
    def _():
        prev_ref[...] = jnp.zeros_like(prev_ref)
        state_ref[...] = jnp.zeros_like(state_ref)

    for cc in range(SSD_STEP_CHUNKS):
        rows = slice(cc * lc, (cc + 1) * lc)
        prev = prev_ref[...] if cc == 0 else xbc_ref[(cc - 1) * lc:cc * lc, :]
        _ssd_chunk(xbc_ref[rows, :], prev, z_ref[rows, :], dt_ref[rows, :], cw_ref, cb_ref, dtb_ref, alog_ref,
                   dskip_ref, og_ref, expand_ref, tri_ref, shift_ref, y_ref.at[rows, :], state_ref)
    prev_ref[...] = xbc_ref[(SSD_STEP_CHUNKS - 1) * lc:, :]


def _ssd_chunk(xb, prev, z, dt, cw_ref, cb_ref, dtb_ref, alog_ref, dskip_ref, og_ref, expand_ref, tri_ref, shift_ref,
               y_ref, state_ref):
    lc = xb.shape[0]
    d_inner = z.shape[1]
    n_heads = d_inner // SSM_HEAD_DIM
    gn = SSM_GROUPS * SSM_D_STATE
    hpg = n_heads // SSM_GROUPS
    gw = hpg * SSM_HEAD_DIM
    n_tap = cw_ref.shape[0]

    sh = _dot(shift_ref[...], jnp.concatenate([prev, xb], axis=0))
    acc = cb_ref[...] + cw_ref[n_tap - 1:n_tap, :] * xb.astype(F32)
    for j in range(1, n_tap):
        acc = acc + cw_ref[n_tap - 1 - j:n_tap - j, :] * sh[(j - 1) * lc:j * lc, :]
    u = _silu(acc)
    xs = u[:, :d_inner]
    bm = u[:, d_inner:d_inner + gn].astype(BF16)
    cm = u[:, d_inner + gn:].astype(BF16)

    dtr = dt + dtb_ref[...]
    dt_c = jnp.maximum(dtr, 0.0) + jnp.log1p(jnp.exp(-jnp.abs(dtr)))
    a_c = dt_c * (-jnp.exp(alog_ref[...]))
    cum = _dot(tri_ref[...], jnp.concatenate(_split3(a_c), axis=1))
    acum_c = (cum[:, :LANES] + cum[:, LANES:2 * LANES]) + cum[:, 2 * LANES:]
    acum_r = acum_c.T
    ex = _dot(jnp.concatenate(_split3(dt_c) + _split3(acum_c), axis=0), expand_ref[...])
    dt_e = (ex[:lc] + ex[lc:2 * lc]) + ex[2 * lc:3 * lc]
    acum_e = (ex[3 * lc:4 * lc] + ex[4 * lc:5 * lc]) + ex[5 * lc:]
    xdt = xs * dt_e

    li = lax.broadcasted_iota(I32, (lc, lc), 0)
    si = lax.broadcasted_iota(I32, (lc, lc), 1)
    causal = li >= si
    mats = []
    for g in range(SSM_GROUPS):
        cb = lax.dot_general(cm[:, g * SSM_D_STATE:(g + 1) * SSM_D_STATE], bm[:, g * SSM_D_STATE:(g + 1) * SSM_D_STATE],
                             (((1,), (1,)), ((), ())), preferred_element_type=F32)
        for r in range(hpg):
            h = g * hpg + r
            seg = acum_c[:, h:h + 1] - acum_r[h:h + 1, :]
            decay = jnp.exp(jnp.where(causal, seg, -jnp.inf))
            mats.append((cb * decay).astype(BF16))
    xdt_b = xdt.astype(BF16)
    row_head = lax.broadcasted_iota(I32, (lc, d_inner), 1) // SSM_HEAD_DIM
    zero = jnp.zeros_like(xdt_b)
    xbd = jnp.concatenate([jnp.where(row_head == h, xdt_b, zero) for h in range(n_heads)], axis=0)
    y = _dot(jnp.concatenate(mats, axis=1), xbd)

    a_last = acum_e[lc - 1:lc, :]
    w_state = (xdt * jnp.exp(a_last - acum_e)).astype(BF16)
    y_off = []
    for g in range(SSM_GROUPS):
        ns = slice(g * SSM_D_STATE, (g + 1) * SSM_D_STATE)
        hs = slice(g * gw, (g + 1) * gw)
        h_in = state_ref[:, hs]
        y_off.append(_dot(cm[:, ns], h_in.astype(BF16)))
        new = lax.dot_general(bm[:, ns], w_state[:, hs], (((0,), (0,)), ((), ())), preferred_element_type=F32)
        state_ref[:, hs] = h_in * jnp.exp(a_last[:, hs]) + new
    y = y + jnp.concatenate(y_off, axis=1) * jnp.exp(acum_e)

    y = y + dskip_ref[...] * xs
    y = y * _silu(z.astype(F32))
    y_ref[...] = _rms(y, og_ref[...]).astype(BF16)


def _ssd(xbc, z, dt, cw, cb, dtb, alog, dskip, og, expand, tri, shift, bsz, seq):
    rows = SSM_CHUNK * SSD_STEP_CHUNKS
    ns = seq // rows
    blk = lambda n: pl.BlockSpec((rows, n), lambda b, c: (b * ns + c, 0))
    full = lambda a: pl.BlockSpec(a.shape, lambda b, c: (0,) * a.ndim)
    d_inner = z.shape[1]
    return pl.pallas_call(
        _ssd_kernel,
        grid=(bsz, ns),
        in_specs=[blk(xbc.shape[1]), blk(d_inner), blk(dt.shape[1])]
        + [full(a) for a in (cw, cb, dtb, alog, dskip, og, expand, tri, shift)],
        out_specs=blk(d_inner),
        out_shape=jax.ShapeDtypeStruct(z.shape, BF16),
        scratch_shapes=[pltpu.VMEM((SSM_CHUNK, xbc.shape[1]), BF16), pltpu.VMEM((SSM_D_STATE, d_inner), F32)],
        compiler_params=_cparams("arbitrary", "arbitrary"),
        name="ssd_scan",
    )(xbc, z, dt, cw, cb, dtb, alog, dskip, og, expand, tri, shift)


def _sorted_rows(tm, n_experts):
    return MOE_TOP_K * tm + n_experts * RUN_ROWS


def _proj_router_kernel(n_act, n_groups, n_experts, *refs):
    x_ref = refs[0]
    act_refs = refs[1:1 + n_act]
    w_refs = refs[1 + n_act:1 + 2 * n_act]
    b_ref, g_ref, wr_ref, br_ref, tri_ref, ltri_ref = refs[1 + 2 * n_act:7 + 2 * n_act]
    xnew_ref, xs_ref, gates_ref, counts_ref = refs[7 + 2 * n_act:]
    tm = x_ref.shape[0]
    epg = n_experts // n_groups

    y = x_ref[...] + b_ref[...]
    for a_ref, w_ref in zip(act_refs, w_refs):
        y = y + _dot(a_ref[...], w_ref[...])
    xnew_ref[...] = y
    xn = _rms(y, g_ref[...]).astype(BF16)

    logits = _dot(xn, wr_ref[...]) + br_ref[...]
    lane = lax.broadcasted_iota(I32, logits.shape, 1)
    big = jnp.int32(LANES)
    neg = -jnp.inf

    def first_argmax(v):
        m = jnp.max(v, axis=-1, keepdims=True)
        return m, jnp.min(jnp.where(v == m, lane, big), axis=-1, keepdims=True)

    gl = jnp.where(lane < n_groups, logits, neg)
    gmax, gsel = first_argmax(gl)
    g_w = 1.0 / jnp.sum(jnp.exp(gl - gmax), axis=-1, keepdims=True)
    elane = lane - n_groups
    in_group = (elane >= 0) & (elane < n_experts) & ((elane // epg) == gsel)
    el = jnp.where(in_group, logits, neg)
    top1, i1 = first_argmax(el)
    top2, i2 = first_argmax(jnp.where(lane == i1, neg, el))
    e2 = jnp.exp(top2 - top1)
    w1 = 1.0 / (1.0 + e2)
    gate1 = g_w * w1
    gate2 = g_w * (e2 * w1)

    sel1 = lane == i1
    sel2 = lane == i2
    onehot = jnp.where(sel1, 1.0, jnp.where(sel2, 1.0, 0.0))
    before = _dot(tri_ref[...], onehot.astype(BF16))
    cnt = jnp.sum(onehot, axis=0, keepdims=True)
    counts_ref[...] = cnt
    cnt_pad = jnp.floor((cnt + (RUN_ROWS - 1)) * (1.0 / RUN_ROWS)) * RUN_ROWS
    run_start = _dot(jnp.broadcast_to(cnt_pad, (SUBLANES, LANES)).astype(BF16), ltri_ref[...])[0:1, :]
    slot = run_start + before
    j1 = jnp.sum(jnp.where(sel1, slot, 0.0), axis=-1, keepdims=True)
    j2 = jnp.sum(jnp.where(sel2, slot, 0.0), axis=-1, keepdims=True)
    table = jnp.where(lane == 0, gate1, jnp.where(lane == 1, gate2,
                                                  jnp.where(lane == 2, j1, jnp.where(lane == 3, j2, 0.0))))
    gates_ref[...] = table

    tt = table.T
    j1r = tt[2:3, :].astype(I32)
    j2r = tt[3:4, :].astype(I32)
    ri = lax.broadcasted_iota(I32, (xs_ref.shape[0], tm), 0)
    perm = jnp.where(ri == j1r, 1.0, jnp.where(ri == j2r, 1.0, 0.0)).astype(BF16)
    xs_ref[...] = _pack_halves(_dot(perm, xn), is_bf16=True)


def _proj_router(x2, acts, ws, bias, g, wr, br, tri, ltri, n_groups, n_experts):
    t, d = x2.shape
    tm = ROW_TILE
    n_tiles = t // tm
    lr = _sorted_rows(tm, n_experts)
    row = lambda n: pl.BlockSpec((tm, n), lambda i: (i, 0))
    full = lambda a: pl.BlockSpec(a.shape, lambda i: (0,) * a.ndim)
    kern = functools.partial(_proj_router_kernel, len(acts), n_groups, n_experts)
    return pl.pallas_call(
        kern,
        grid=(n_tiles,),
        in_specs=[row(d)] + [row(a.shape[1]) for a in acts] + [full(a) for a in (*ws, bias, g, wr, br, tri, ltri)],
        out_specs=[row(d), pl.BlockSpec((lr, d // 2), lambda i: (i, 0)), row(LANES),
                   pl.BlockSpec((None, 1, LANES), lambda i: (i, 0, 0))],
        out_shape=[jax.ShapeDtypeStruct((t, d), F32), jax.ShapeDtypeStruct((n_tiles * lr, d // 2), U32),
                   jax.ShapeDtypeStruct((t, LANES), F32), jax.ShapeDtypeStruct((n_tiles, 1, LANES), F32)],
        compiler_params=_cparams("parallel"),
        name="proj_router",
    )(x2, *acts, *ws, bias, g, wr, br, tri, ltri)


def _group_copies(table_ref, base, n, src_ref, dst_ref, sem):
    for q in range(n):
        g = table_ref[base + q]
        pltpu.make_async_copy(src_ref.at[pl.ds(pl.multiple_of(g * RUN_ROWS, RUN_ROWS), RUN_ROWS)],
                              dst_ref.at[pl.ds(q * RUN_ROWS, RUN_ROWS)], sem).start(priority=q % 2)


def _experts_kernel(be_ref, nused_ref, src_ref, xs_ref, wgu_ref, wd_ref, y_ref, xbuf, sems, wgu_b, wd_b):
    i = pl.program_id(0)
    n_used = nused_ref[0]
    rows = y_ref.shape[0]
    gpb = rows // RUN_ROWS
    slot = i % 2

    def fetch(blk, s):
        _group_copies(src_ref, blk * gpb, gpb, xs_ref, xbuf.at[s], sems.at[s])

    def wait(s):
        pltpu.make_async_copy(xs_ref.at[pl.ds(0, rows)], xbuf.at[s], sems.at[s]).wait()

    @pl.when(i == 0)
    def _():
        fetch(0, 0)

    @pl.when((i == 0) | (be_ref[i] != be_ref[jnp.maximum(i - 1, 0)]))
    def _():
        wgu_b[...] = wgu_ref[...].astype(BF16)
        wd_b[...] = wd_ref[...].astype(BF16)

    @pl.when(i >= n_used)
    def _():
        y_ref[...] = jnp.zeros_like(y_ref)

    @pl.when(i == n_used)
    def _():
        wait(slot)

    @pl.when(i < n_used)
    def _():
        wait(slot)
        fetch(i + 1, 1 - slot)
        ff = wd_b.shape[0]
        x = _unpack_halves(xbuf[slot]).astype(BF16)
        gu = _dot(x, wgu_b[...])
        h = (_silu(gu[:, :ff]) * gu[:, ff:]).astype(BF16)
        y_ref[...] = _pack_halves(_dot(h, wd_b[...]))


def _experts(block_expert, n_used, src_groups, xs, wgu, wd, layer, n_blocks):
    w = xs.shape[1]
    d, ff2 = wgu.shape[2:]
    rows = EXPERT_ROWS
    grid_spec = pltpu.PrefetchScalarGridSpec(
        num_scalar_prefetch=3,
        grid=(n_blocks,),
        in_specs=[pl.BlockSpec(memory_space=pl.ANY),
                  pl.BlockSpec((None, None, d, ff2), lambda i, be, nu, sg: (layer, be[i], 0, 0)),
                  pl.BlockSpec((None, None, ff2 // 2, d), lambda i, be, nu, sg: (layer, be[i], 0, 0))],
        out_specs=pl.BlockSpec((rows, w), lambda i, be, nu, sg: (i, 0)),
        scratch_shapes=[pltpu.VMEM((2, rows, w), U32), pltpu.SemaphoreType.DMA((2,)),
                        pltpu.VMEM((d, ff2), BF16), pltpu.VMEM((ff2 // 2, d), BF16)],
    )
    return pl.pallas_call(
        _experts_kernel,
        grid_spec=grid_spec,
        out_shape=jax.ShapeDtypeStruct((n_blocks * rows, w), U32),
        compiler_params=_cparams("arbitrary"),
        name="moe_experts",
    )(block_expert, n_used, src_groups, xs, wgu, wd)


def _combine_kernel(wide_ref, nwide_ref, single_ref, nsingle_ref, x_ref, gates_ref, y_ref, o_ref, ybuf, sems):
    i = pl.program_id(0)
    n_tiles = pl.num_programs(0)
    tm = x_ref.shape[0]
    lr = ybuf.shape[1]
    max_wide = wide_ref.shape[0] // nwide_ref.shape[0]
    max_single = single_ref.shape[0] // nsingle_ref.shape[0]
    slot = i % 2

    def fetch(tile, s):
        def copies(tab_ref, base, n, groups, priority):
            def start(p, carry):
                code = tab_ref[base + p]
                src = pl.multiple_of(lax.shift_right_logical(code, COPY_DST_BITS) * RUN_ROWS, RUN_ROWS)
                dst = pl.multiple_of((code & ((1 << COPY_DST_BITS) - 1)) * RUN_ROWS, RUN_ROWS)
                pltpu.make_async_copy(y_ref.at[pl.ds(src, groups * RUN_ROWS)],
                                      ybuf.at[s, pl.ds(dst, groups * RUN_ROWS)], sems.at[s]).start(priority=priority)
                return carry
            lax.fori_loop(0, n, start, 0)
        copies(wide_ref, tile * max_wide, nwide_ref[tile], COPY_GROUPS, 0)
        copies(single_ref, tile * max_single, nsingle_ref[tile], 1, 1)

    def wait(s):
        pltpu.make_async_copy(y_ref.at[pl.ds(0, lr)], ybuf.at[s], sems.at[s]).wait()

    @pl.when(i == 0)
    def _():
        fetch(0, 0)

    wait(slot)
    fetch(jnp.minimum(i + 1, n_tiles - 1), 1 - slot)
    ys = _unpack_halves(ybuf[slot]).astype(BF16)
    table = gates_ref[...]
    col = lax.broadcasted_iota(I32, (tm, lr), 1)
    pick = jnp.zeros((tm, lr), F32)
    for k in range(MOE_TOP_K):
        jk = table[:, MOE_TOP_K + k:MOE_TOP_K + k + 1].astype(I32)
        pick = jnp.where(col == jk, table[:, k:k + 1], pick)
    o_ref[...] = x_ref[...] + _dot(pick.astype(BF16), ys)

    @pl.when(i == n_tiles - 1)
    def _():
        wait(1 - slot)


def _combine(copy_lists, xnew, gates, y, lr):
    t, d = xnew.shape
    tm = ROW_TILE
    grid_spec = pltpu.PrefetchScalarGridSpec(
        num_scalar_prefetch=len(copy_lists),
        grid=(t // tm,),
        in_specs=[pl.BlockSpec((tm, d), lambda i, *p: (i, 0)),
                  pl.BlockSpec((tm, LANES), lambda i, *p: (i, 0)),
                  pl.BlockSpec(memory_space=pl.ANY)],
        out_specs=pl.BlockSpec((tm, d), lambda i, *p: (i, 0)),
        scratch_shapes=[pltpu.VMEM((2, lr, y.shape[1]), U32), pltpu.SemaphoreType.DMA((2,))],
    )
    return pl.pallas_call(
        _combine_kernel,
        grid_spec=grid_spec,
        out_shape=jax.ShapeDtypeStruct((t, d), F32),
        compiler_params=_cparams("arbitrary"),
        name="moe_combine",
    )(*copy_lists, xnew, gates, y)


def _moe_plan(counts, n_groups, n_experts, lr):
    n_tiles = counts.shape[0]
    gpb = EXPERT_ROWS // RUN_ROWS
    gpt = lr // RUN_ROWS
    cnt = counts[:, 0, n_groups:n_groups + n_experts].astype(I32)
    run = (cnt + RUN_ROWS - 1) // RUN_ROWS
    run_end = jnp.cumsum(run, axis=1)
    run_off = run_end - run
    used = run_end[:, -1]
    seg = jnp.sum(run, axis=0)
    seg_pad = (seg + gpb - 1) // gpb * gpb
    seg_end = jnp.cumsum(seg_pad)
    seg_start = seg_end - seg_pad
    before = jnp.cumsum(run, axis=0) - run
    n_blocks = (n_tiles * gpt + n_experts * (gpb - 1)) // gpb + 1
    n_used = (seg_end[-1] // gpb).astype(I32)
    blk = jnp.arange(n_blocks, dtype=I32)
    first = jnp.minimum(blk, n_used - 1) * gpb
    block_expert = jnp.minimum(jnp.sum(seg_end[None, :] <= first[:, None], axis=1), n_experts - 1).astype(I32)

    assert gpt < (1 << COPY_DST_BITS) and gpt - MOE_TOP_K * ROW_TILE // RUN_ROWS <= gpb
    length = jnp.concatenate([run, (gpt - used)[:, None]], axis=1)
    local = jnp.concatenate([run_off, used[:, None]], axis=1)
    glob = jnp.concatenate([seg_start[None, :] + before, jnp.full((n_tiles, 1), (n_blocks - 1) * gpb, I32)], axis=1)
    n_runs = length.shape[1]
    wide = length // COPY_GROUPS

    def copy_list(per_run, first, n_slots, step):
        end = jnp.cumsum(per_run, axis=1)
        p = jnp.arange(n_slots, dtype=I32)
        owner = jnp.minimum(jnp.sum(end[:, None, :] <= p[None, :, None], axis=2), n_runs - 1)
        own = (owner[:, :, None] == jnp.arange(n_runs, dtype=I32)[None, None, :]).astype(I32)
        sel = lambda tab: jnp.sum(own * tab[:, None, :], axis=2)
        off = sel(first) + (p[None, :] - sel(end - per_run)) * step
        code = ((sel(glob) + off) << COPY_DST_BITS) + sel(local) + off
        return code.astype(I32).reshape(-1), end[:, -1].astype(I32)

    wide_list = copy_list(wide, jnp.zeros_like(wide), gpt // COPY_GROUPS, COPY_GROUPS)
    single_list = copy_list(length - wide * COPY_GROUPS, wide * COPY_GROUPS, n_runs * (COPY_GROUPS - 1), 1)
    copy_lists = (*wide_list, *single_list)

    pick = lambda onehot, tab: jnp.dot(onehot, tab.astype(F32), precision=lax.Precision.HIGHEST)
    zero_src = gpt - 1
    g = jnp.arange(n_blocks * gpb, dtype=I32)
    ge = jnp.minimum(jnp.sum(seg_end[None, :] <= g[:, None], axis=1), n_experts - 1)
    oh_e = (ge[:, None] == jnp.arange(n_experts, dtype=I32)[None, :]).astype(F32)
    per_e = pick(oh_e, jnp.stack([seg_start, seg], axis=1))
    m = g.astype(F32) - per_e[:, 0]
    tile_end = pick(oh_e, jnp.cumsum(run, axis=0).T)
    tile = jnp.minimum(jnp.sum(tile_end <= m[:, None], axis=1), n_tiles - 1)
    oh_t = (tile[:, None] == jnp.arange(n_tiles, dtype=I32)[None, :]).astype(F32)
    tile_base = (jnp.arange(n_tiles, dtype=I32)[:, None] * gpt + run_off - before).T
    src = jnp.sum(oh_t * pick(oh_e, tile_base), axis=1) + m
    src = jnp.where(m < per_e[:, 1], src, float(zero_src)).astype(I32)
    return block_expert, n_used.reshape(1), src, copy_lists, n_blocks


def _moe(xnew, xs, gates, counts, wgu, wd, layer, n_groups):
    n_experts = wgu.shape[1]
    lr = _sorted_rows(ROW_TILE, n_experts)
    block_expert, n_used, src, copy_lists, n_blocks = _moe_plan(counts, n_groups, n_experts, lr)
    y = _experts(block_expert, n_used, src, xs, wgu, wd, layer, n_blocks)
    return _combine(copy_lists, xnew, gates, y, lr)


def _pw1_kernel(x_ref, g_ref, w_ref, b_ref, u_ref):
    ch = u_ref.shape[1]
    xn = _rms(x_ref[...], g_ref[...]).astype(BF16)
    y = _dot(xn, w_ref[...]) + b_ref[...]
    u_ref[...] = (y[:, :ch] * jax.nn.sigmoid(y[:, ch:])).astype(BF16)


def _pw1(x2, g, w, b):
    t, d = x2.shape
    tm = ROW_TILE
    ch = w.shape[1] // 2
    full = lambda a: pl.BlockSpec(a.shape, lambda i: (0,) * a.ndim)
    return pl.pallas_call(
        _pw1_kernel,
        grid=(t // tm,),
        in_specs=[pl.BlockSpec((tm, d), lambda i: (i, 0)), full(g), full(w), full(b)],
        out_specs=pl.BlockSpec((tm, ch), lambda i: (i, 0)),
        out_shape=jax.ShapeDtypeStruct((t, ch), BF16),
        compiler_params=_cparams("parallel"),
        name="pw1_glu",
    )(x2, g, w, b)


def _dwconv_kernel(cur_ref, halo_ref, w_ref, b_ref, lg_ref, lb_ref, o_ref, xs_ref, acc_ref):
    ts, ch = cur_ref.shape
    n_strip, n_tap = w_ref.shape[:2]
    n_chunk = ts // CONV_ROWS
    first_tap = CONV_HALO - (n_tap - 1)
    halo = halo_ref[...].astype(F32)
    halo = jnp.where(pl.program_id(1) > 0, halo, jnp.zeros_like(halo))
    cur = cur_ref[...].astype(F32)
    shifted_rows = ts + CONV_HALO - SUBLANES
    for c in range(n_strip):
        cs = slice(c * LANES, (c + 1) * LANES)
        xs_ref[0, c, :CONV_HALO, :] = halo[:, cs]
        xs_ref[0, c, CONV_HALO:, :] = cur[:, cs]
        for s in range(1, SUBLANES):
            xs_ref[s, c, :shifted_rows, :] = xs_ref[0, c, s:s + shifted_rows, :]

    def strip_chunk(idx, carry):
        r = idx // n_strip
        c = idx % n_strip
        row0 = pl.multiple_of(r * CONV_ROWS, CONV_ROWS)
        acc = jnp.broadcast_to(b_ref[c], (CONV_ROWS, LANES))
        for s in range(SUBLANES):
            taps = [k for k in range(n_tap) if (first_tap + k) % SUBLANES == s]
            lo = (first_tap + taps[0]) // SUBLANES * SUBLANES
            hi = (first_tap + taps[-1]) // SUBLANES * SUBLANES
            data = xs_ref[s, c, pl.ds(row0 + lo, hi - lo + CONV_ROWS), :]
            for k in taps:
                a = (first_tap + k) // SUBLANES * SUBLANES - lo
                acc = acc + w_ref[c, k:k + 1, :] * data[a:a + CONV_ROWS, :]
        acc_ref[c, pl.ds(row0, CONV_ROWS), :] = acc
        return carry

    lax.fori_loop(0, n_chunk * n_strip, strip_chunk, 0, unroll=2)
    y = jnp.concatenate([acc_ref[c] for c in range(n_strip)], axis=1)
    mu = jnp.mean(y, axis=-1, keepdims=True)
    cen = y - mu
    var = jnp.mean(cen * cen, axis=-1, keepdims=True)
    v = cen * lax.rsqrt(var + LN_EPS) * lg_ref[...] + lb_ref[...]
    o_ref[...] = _silu(v).astype(BF16)


def _dwconv(u, w, b, lg, lb, bsz, seq):
    ts = CONV_TILE
    nt = seq // ts
    ch = u.shape[1]
    hpt = ts // CONV_HALO
    n_strip = ch // LANES
    full = lambda a: pl.BlockSpec(a.shape, lambda bb, i: (0,) * a.ndim)
    w = w.reshape(-1, n_strip, LANES).transpose(1, 0, 2)
    b = b.reshape(n_strip, 1, LANES)
    return pl.pallas_call(
        _dwconv_kernel,
        grid=(bsz, nt),
        in_specs=[pl.BlockSpec((ts, ch), lambda bb, i: (bb * nt + i, 0)),
                  pl.BlockSpec((CONV_HALO, ch), lambda bb, i: (jnp.maximum((bb * nt + i) * hpt - 1, 0), 0)),
                  full(w), full(b), full(lg), full(lb)],
        out_specs=pl.BlockSpec((ts, ch), lambda bb, i: (bb * nt + i, 0)),
        out_shape=jax.ShapeDtypeStruct(u.shape, BF16),
        scratch_shapes=[pltpu.VMEM((SUBLANES, n_strip, CONV_HALO + ts, LANES), F32),
                        pltpu.VMEM((n_strip, ts, LANES), F32)],
        compiler_params=_cparams("parallel", "parallel"),
        name="dwconv_ln",
    )(u, u, w, b, lg, lb)


def _row(v):
    return v.reshape(1, -1).astype(F32)


def _pad_lanes(v, n=LANES):
    v = v.reshape(1, -1).astype(F32)
    return jnp.pad(v, ((0, 0), (0, n - v.shape[1])))


def _router_params(w_group, b_group, w_expert, b_expert):
    wr = jnp.concatenate([w_group, w_expert], axis=1)
    wr = jnp.pad(wr, ((0, 0), (0, LANES - wr.shape[1]))).astype(BF16)
    br = _pad_lanes(jnp.concatenate([b_group.reshape(-1), b_expert.reshape(-1)]))
    return wr, br


def kernel(x, mix_norm_g, w_in, q_norm_g, k_norm_g, attn_sinks, ssm_conv_w, ssm_conv_b, ssm_dt_bias, ssm_a_log, ssm_d, ssm_out_norm_g, w_out, conv_norm_g, conv_w_pw1, conv_b_pw1, conv_w_dw, conv_b_dw, conv_ln_g, conv_ln_b, conv_w_pw2, conv_b_pw2, moe_norm_g, moe_w_group, moe_b_group, moe_w_expert, moe_b_expert, moe_w_gate_up, moe_w_down):
    bsz, seq, d = x.shape
    t = bsz * seq
    n_heads = attn_sinks.shape[1]
    n_kv = n_heads // GQA_REP
    q_cols = n_heads * HEAD_DIM
    kv_cols = n_kv * HEAD_DIM
    d_inner = ssm_out_norm_g.shape[1]
    ssm_heads = ssm_a_log.shape[1]
    n_groups = moe_w_group.shape[2]
    n_experts = moe_w_expert.shape[2]
    assert t % ROW_TILE == 0 and seq % CONV_TILE == 0 and seq % (SSM_CHUNK * SSD_STEP_CHUNKS) == 0 and seq % (ATTN_BLOCK * ATTN_STEP_BLOCKS) == 0
    assert ssm_heads <= LANES and n_groups + n_experts <= LANES

    x2 = x.reshape(t, d)
    tri_strict = jnp.asarray(np.tril(np.ones((ROW_TILE, ROW_TILE), np.float32), -1), BF16)
    lane_before = jnp.asarray(np.triu(np.ones((LANES, LANES), np.float32), 1), BF16)
    zero_bias = jnp.zeros((1, d), F32)

    w = w_in[0]
    cuts = [0] + np.cumsum([q_cols, kv_cols, kv_cols, d_inner, ssm_conv_w.shape[2]]).tolist() + [w.shape[1]]
    wq, wk, wv, wz, wx, wdt = (w[:, lo:hi] for lo, hi in zip(cuts[:-1], cuts[1:]))
    wdt = jnp.pad(wdt, ((0, 0), (0, LANES - ssm_heads)))
    seg_ones = jnp.asarray(np.kron(np.eye(q_cols // HEAD_DIM, dtype=np.float32),
                                   np.ones((HEAD_DIM, HEAD_DIM), np.float32)), BF16)
    qg = jnp.tile(_row(q_norm_g[0]), (1, n_heads))
    kg = jnp.tile(_row(k_norm_g[0]), (1, n_kv))
    q, k, v, z, xbc, dt = _inproj(
        x2, _row(mix_norm_g[0]), wq.astype(BF16), wk.astype(BF16), wv.astype(BF16), wz.astype(BF16),
        wx.astype(BF16), wdt.astype(BF16), qg, kg, seg_ones)
    kv_rep = jnp.asarray(np.stack([np.kron(np.eye(n_kv, dtype=np.float32)[:, g:g + 1],
                                           np.tile(np.eye(HEAD_DIM, dtype=np.float32), (1, GQA_REP)))
                                   for g in range(n_kv)]), BF16)
    y_attn = _attention(attn_sinks[0].reshape(1, n_heads).astype(F32), kv_rep, q, k, v, bsz, seq)
    expand = jnp.asarray(np.kron(np.eye(LANES, ssm_heads, dtype=np.float32),
                                 np.ones((1, SSM_HEAD_DIM), np.float32)), BF16)
    tri_incl = jnp.asarray(np.tril(np.ones((SSM_CHUNK, SSM_CHUNK), np.float32)), BF16)
    n_tap = ssm_conv_w.shape[1]
    shift = jnp.asarray(np.concatenate([np.eye(SSM_CHUNK, 2 * SSM_CHUNK, SSM_CHUNK - j, dtype=np.float32)
                                        for j in range(1, n_tap)], axis=0), BF16)
    y_ssm = _ssd(xbc, z, dt, ssm_conv_w[0].astype(F32), _row(ssm_conv_b[0]), _pad_lanes(ssm_dt_bias[0]),
                 _pad_lanes(ssm_a_log[0]), jnp.repeat(_row(ssm_d[0]), SSM_HEAD_DIM, axis=1), _row(ssm_out_norm_g[0]),
                 expand, tri_incl, shift, bsz, seq)
    wo = w_out[0].astype(BF16)
    wr, br = _router_params(moe_w_group[0], moe_b_group[0], moe_w_expert[0], moe_b_expert[0])
    xnew, xs, gates, counts = _proj_router(
        x2, [y_attn, y_ssm], [wo[:q_cols], wo[q_cols:]], zero_bias, _row(moe_norm_g[0]), wr, br, tri_strict,
        lane_before, n_groups, n_experts)
    x2 = _moe(xnew, xs, gates, counts, moe_w_gate_up, moe_w_down, 0, n_groups)

    u = _pw1(x2, _row(conv_norm_g[0]), conv_w_pw1[0].astype(BF16), _row(conv_b_pw1[0]))
    u = _dwconv(u, conv_w_dw[0].astype(F32), _row(conv_b_dw[0]), _row(conv_ln_g[0]), _row(conv_ln_b[0]), bsz, seq)
    wr, br = _router_params(moe_w_group[1], moe_b_group[1], moe_w_expert[1], moe_b_expert[1])
    xnew, xs, gates, counts = _proj_router(
        x2, [u], [conv_w_pw2[0].astype(BF16)], _row(conv_b_pw2[0]), _row(moe_norm_g[1]), wr, br, tri_strict,
        lane_before, n_groups, n_experts)
    x2 = _moe(xnew, xs, gates, counts, moe_w_gate_up, moe_w_down, 1, n_groups)
    return x2.reshape(bsz, seq, d)
```

```python
import functools

import numpy as np
import jax
import jax.numpy as jnp
from jax import lax
from jax.experimental import pallas as pl
from jax.experimental.pallas import tpu as pltpu

F32 = jnp.float32
BF16 = jnp.bfloat16
I32 = jnp.int32
U32 = jnp.uint32

HEAD_DIM = 64
GQA_REP = 4
ATTN_BLOCK = 128
SSM_HEAD_DIM = 64
SSM_GROUPS = 2
SSM_D_STATE = 128
SSM_CHUNK = 128
MOE_TOP_K = 2
RMS_EPS = 1e-6
LN_EPS = 1e-5

LANES = 128
SUBLANES = 8

ROW_TILE = 512
EXPERT_ROWS = 512
SSD_STEP_CHUNKS = 2
ATTN_STEP_BLOCKS = 2
CONV_TILE = 512
CONV_HALO = 32
CONV_ROWS = 64
RUN_ROWS = SUBLANES
COPY_GROUPS = 4
COPY_DST_BITS = 8
VMEM_LIMIT = 48 * 1024 * 1024


def _cparams(*sem):
    return pltpu.CompilerParams(dimension_semantics=sem, vmem_limit_bytes=VMEM_LIMIT)


def _dot(a, b):
    return jnp.dot(a, b, preferred_element_type=F32)


def _rms(x, g):
    return x * lax.rsqrt(jnp.mean(x * x, axis=-1, keepdims=True) + RMS_EPS) * g


def _silu(x):
    return x * jax.nn.sigmoid(x)


def _pack_halves(y, is_bf16=False):
    c = y.shape[1] // 2
    if is_bf16:
        return lax.bitcast_convert_type(y[:, :c], U32) | (lax.bitcast_convert_type(y[:, c:], U32) >> 16)
    hi = lax.bitcast_convert_type(y[:, :c].astype(BF16).astype(F32), U32)
    lo = lax.bitcast_convert_type(y[:, c:].astype(BF16).astype(F32), U32)
    return (hi & jnp.uint32(0xFFFF0000)) | (lo >> 16)


def _unpack_halves(u):
    hi = lax.bitcast_convert_type(u & jnp.uint32(0xFFFF0000), F32)
    lo = lax.bitcast_convert_type(u << 16, F32)
    return jnp.concatenate([hi, lo], axis=1)


def _inproj_kernel(x_ref, g_ref, wq_ref, wk_ref, wv_ref, wz_ref, wx_ref, wdt_ref, qg_ref, kg_ref, bd_ref,
                   q_out, k_out, v_out, z_out, xbc_out, dt_out):
    xn = _rms(x_ref[...], g_ref[...]).astype(BF16)

    def head_rms(y, gain):
        bd = bd_ref[:y.shape[1], :y.shape[1]]
        sq = y * y
        hi = sq.astype(BF16)
        lo = (sq - hi.astype(F32)).astype(BF16)
        ss = _dot(hi, bd) + _dot(lo, bd)
        return y * lax.rsqrt(ss * (1.0 / HEAD_DIM) + RMS_EPS) * gain

    q = head_rms(_dot(xn, wq_ref[...]), qg_ref[...])
    q_out[...] = (q * (HEAD_DIM ** -0.5)).astype(BF16)
    k_out[...] = head_rms(_dot(xn, wk_ref[...]), kg_ref[...]).astype(BF16)
    v_out[...] = _dot(xn, wv_ref[...]).astype(BF16)
    z_out[...] = _dot(xn, wz_ref[...]).astype(BF16)
    xbc_out[...] = _dot(xn, wx_ref[...]).astype(BF16)
    dt_out[...] = _dot(xn, wdt_ref[...])


def _inproj(x2, g, wq, wk, wv, wz, wx, wdt, qg, kg, bd):
    t, d = x2.shape
    tm = ROW_TILE
    row = lambda n: pl.BlockSpec((tm, n), lambda i: (i, 0))
    full = lambda a: pl.BlockSpec(a.shape, lambda i: (0,) * a.ndim)
    outs = [(wq.shape[1], BF16), (wk.shape[1], BF16), (wv.shape[1], BF16), (wz.shape[1], BF16),
            (wx.shape[1], BF16), (wdt.shape[1], F32)]
    return pl.pallas_call(
        _inproj_kernel,
        grid=(t // tm,),
        in_specs=[row(d)] + [full(a) for a in (g, wq, wk, wv, wz, wx, wdt, qg, kg, bd)],
        out_specs=[row(n) for n, _ in outs],
        out_shape=[jax.ShapeDtypeStruct((t, n), dt) for n, dt in outs],
        compiler_params=_cparams("parallel"),
        name="inproj",
    )(x2, g, wq, wk, wv, wz, wx, wdt, qg, kg, bd)


def _attn_kernel(sink_ref, rep_ref, q_ref, kc_ref, kp_ref, vc_ref, vp_ref, o_ref):
    n = pl.program_id(1)
    blk = ATTN_BLOCK
    gw = GQA_REP * HEAD_DIM
    n_kv = q_ref.shape[1] // gw
    qi = lax.broadcasted_iota(I32, (blk, 2 * blk), 0)
    kj = lax.broadcasted_iota(I32, (blk, 2 * blk), 1)
    rel = qi + blk - kj
    in_window = (rel >= 0) & (rel < ATTN_BLOCK)
    first_key = jnp.where(n > 0, 0, blk)
    key_head = lax.broadcasted_iota(I32, (2 * blk, gw), 1) // HEAD_DIM
    out_head = lax.broadcasted_iota(I32, (blk, gw), 1) // HEAD_DIM
    for sb, g in [(sb, g) for sb in range(ATTN_STEP_BLOCKS) for g in range(n_kv)]:
        sl = slice(g * gw, (g + 1) * gw)
        rows = slice(sb * blk, (sb + 1) * blk)
        before = slice((sb - 1) * blk, sb * blk)
        band = in_window & (kj >= first_key) if sb == 0 else in_window
        qg = q_ref[rows, sl]
        kcat = jnp.concatenate([kp_ref[...] if sb == 0 else kc_ref[before, :], kc_ref[rows, :]], axis=0)
        vcat = jnp.concatenate([vp_ref[...] if sb == 0 else vc_ref[before, :], vc_ref[rows, :]], axis=0)
        kk = _dot(kcat, rep_ref[g]).astype(BF16)
        vv = _dot(vcat, rep_ref[g]).astype(BF16)
        zero = jnp.zeros_like(kk)
        kbd = jnp.concatenate([jnp.where(key_head == h, kk, zero) for h in range(GQA_REP)], axis=0)
        vbd = jnp.concatenate([jnp.where(key_head == h, vv, zero) for h in range(GQA_REP)], axis=0)
        s = lax.dot_general(qg, kbd, (((1,), (1,)), ((), ())), preferred_element_type=F32)
        probs, scale = [], None
        for h in range(GQA_REP):
            sh = jnp.where(band, s[:, h * 2 * blk:(h + 1) * 2 * blk], -jnp.inf)
            sink = sink_ref[0, g * GQA_REP + h]
            m = jnp.maximum(jnp.max(sh, axis=-1, keepdims=True), sink)
            p = jnp.exp(sh - m)
            denom = jnp.sum(p, axis=-1, keepdims=True) + jnp.exp(sink - m)
            probs.append(p.astype(BF16))
            inv = jnp.broadcast_to(1.0 / denom, (blk, gw))
            scale = inv if scale is None else jnp.where(out_head == h, inv, scale)
        o = _dot(jnp.concatenate(probs, axis=1), vbd)
        o_ref[rows, sl] = (o * scale).astype(BF16)


def _attention(sinks, kv_rep, q, k, v, bsz, seq):
    blk = ATTN_BLOCK
    spb = ATTN_STEP_BLOCKS
    nb = seq // (blk * spb)
    cur = lambda w: pl.BlockSpec((blk * spb, w), lambda b, n: (b * nb + n, 0))
    prev = lambda w: pl.BlockSpec((blk, w), lambda b, n: (jnp.maximum((b * nb + n) * spb - 1, 0), 0))
    qw, kw = q.shape[1], k.shape[1]
    return pl.pallas_call(
        _attn_kernel,
        grid=(bsz, nb),
        in_specs=[pl.BlockSpec(memory_space=pltpu.SMEM), pl.BlockSpec(kv_rep.shape, lambda b, n: (0, 0, 0)),
                  cur(qw), cur(kw), prev(kw), cur(kw), prev(kw)],
        out_specs=cur(qw),
        out_shape=jax.ShapeDtypeStruct(q.shape, BF16),
        compiler_params=_cparams("parallel", "parallel"),
        name="swa_attention",
    )(sinks, kv_rep, q, k, k, v, v)


def _split3(a):
    a1 = a.astype(BF16)
    r = a - a1.astype(F32)
    a2 = r.astype(BF16)
    return a1, a2, (r - a2.astype(F32)).astype(BF16)


def _ssd_kernel(xbc_ref, z_ref, dt_ref, cw_ref, cb_ref, dtb_ref, alog_ref, dskip_ref, og_ref, expand_ref, tri_ref,
                shift_ref, y_ref, prev_ref, state_ref):
    lc = SSM_CHUNK

    @pl.when(pl.program_id(1) == 0)
    def _():
        prev_ref[...] = jnp.zeros_like(prev_ref)
        state_ref[...] = jnp.zeros_like(state_ref)

    for cc in range(SSD_STEP_CHUNKS):
        rows = slice(cc * lc, (cc + 1) * lc)
        prev = prev_ref[...] if cc == 0 else xbc_ref[(cc - 1) * lc:cc * lc, :]
        _ssd_chunk(xbc_ref[rows, :], prev, z_ref[rows, :], dt_ref[rows, :], cw_ref, cb_ref, dtb_ref, alog_ref,
                   dskip_ref, og_ref, expand_ref, tri_ref, shift_ref, y_ref.at[rows, :], state_ref)
    prev_ref[...] = xbc_ref[(SSD_STEP_CHUNKS - 1) * lc:, :]


def _ssd_chunk(xb, prev, z, dt, cw_ref, cb_ref, dtb_ref, alog_ref, dskip_ref, og_ref, expand_ref, tri_ref, shift_ref,
               y_ref, state_ref):
    lc = xb.shape[0]
    d_inner = z.shape[1]
    n_heads = d_inner // SSM_HEAD_DIM
    gn = SSM_GROUPS * SSM_D_STATE
    hpg = n_heads // SSM_GROUPS
    gw = hpg * SSM_HEAD_DIM
    n_tap = cw_ref.shape[0]

    sh = _dot(shift_ref[...], jnp.concatenate([prev, xb], axis=0))
    acc = cb_ref[...] + cw_ref[n_tap - 1:n_tap, :] * xb.astype(F32)
    for j in range(1, n_tap):
        acc = acc + cw_ref[n_tap - 1 - j:n_tap - j, :] * sh[(j - 1) * lc:j * lc, :]
    u = _silu(acc)
    xs = u[:, :d_inner]
    bm = u[:, d_inner:d_inner + gn].astype(BF16)
    cm = u[:, d_inner + gn:].astype(BF16)

    dtr = dt + dtb_ref[...]
    dt_c = jnp.maximum(dtr, 0.0) + jnp.log1p(jnp.exp(-jnp.abs(dtr)))
    a_c = dt_c * (-jnp.exp(alog_ref[...]))
    cum = _dot(tri_ref[...], jnp.concatenate(_split3(a_c), axis=1))
    acum_c = (cum[:, :LANES] + cum[:, LANES:2 * LANES]) + cum[:, 2 * LANES:]
    acum_r = acum_c.T
    ex = _dot(jnp.concatenate(_split3(dt_c) + _split3(acum_c), axis=0), expand_ref[...])
    dt_e = (ex[:lc] + ex[lc:2 * lc]) + ex[2 * lc:3 * lc]
    acum_e = (ex[3 * lc:4 * lc] + ex[4 * lc:5 * lc]) + ex[5 * lc:]
    xdt = xs * dt_e

    li = lax.broadcasted_iota(I32, (lc, lc), 0)
    si = lax.broadcasted_iota(I32, (lc, lc), 1)
    causal = li >= si
    mats = []
    for g in range(SSM_GROUPS):
        cb = lax.dot_general(cm[:, g * SSM_D_STATE:(g + 1) * SSM_D_STATE], bm[:, g * SSM_D_STATE:(g + 1) * SSM_D_STATE],
                             (((1,), (1,)), ((), ())), preferred_element_type=F32)
        for r in range(hpg):
            h = g * hpg + r
            seg = acum_c[:, h:h + 1] - acum_r[h:h + 1, :]
            decay = jnp.exp(jnp.where(causal, seg, -jnp.inf))
            mats.append((cb * decay).astype(BF16))
    xdt_b = xdt.astype(BF16)
    row_head = lax.broadcasted_iota(I32, (lc, d_inner), 1) // SSM_HEAD_DIM
    zero = jnp.zeros_like(xdt_b)
    xbd = jnp.concatenate([jnp.where(row_head == h, xdt_b, zero) for h in range(n_heads)], axis=0)
    y = _dot(jnp.concatenate(mats, axis=1), xbd)

    a_last = acum_e[lc - 1:lc, :]
    w_state = (xdt * jnp.exp(a_last - acum_e)).astype(BF16)
    y_off = []
    for g in range(SSM_GROUPS):
        ns = slice(g * SSM_D_STATE, (g + 1) * SSM_D_STATE)
        hs = slice(g * gw, (g + 1) * gw)
        h_in = state_ref[:, hs]
        y_off.append(_dot(cm[:, ns], h_in.astype(BF16)))
        new = lax.dot_general(bm[:, ns], w_state[:, hs], (((0,), (0,)), ((), ())), preferred_element_type=F32)
        state_ref[:, hs] = h_in * jnp.exp(a_last[:, hs]) + new
    y = y + jnp.concatenate(y_off, axis=1) * jnp.exp(acum_e)

    y = y + dskip_ref[...] * xs
    y = y * _silu(z.astype(F32))
    y_ref[...] = _rms(y, og_ref[...]).astype(BF16)


def _ssd(xbc, z, dt, cw, cb, dtb, alog, dskip, og, expand, tri, shift, bsz, seq):
    rows = SSM_CHUNK * SSD_STEP_CHUNKS
    ns = seq // rows
    blk = lambda n: pl.BlockSpec((rows, n), lambda b, c: (b * ns + c, 0))
    full = lambda a: pl.BlockSpec(a.shape, lambda b, c: (0,) * a.ndim)
    d_inner = z.shape[1]
    return pl.pallas_call(
        _ssd_kernel,
        grid=(bsz, ns),
        in_specs=[blk(xbc.shape[1]), blk(d_inner), blk(dt.shape[1])]
        + [full(a) for a in (cw, cb, dtb, alog, dskip, og, expand, tri, shift)],
        out_specs=blk(d_inner),
        out_shape=jax.ShapeDtypeStruct(z.shape, BF16),
        scratch_shapes=[pltpu.VMEM((SSM_CHUNK, xbc.shape[1]), BF16), pltpu.VMEM((SSM_D_STATE, d_inner), F32)],
        compiler_params=_cparams("arbitrary", "arbitrary"),
        name="ssd_scan",
    )(xbc, z, dt, cw, cb, dtb, alog, dskip, og, expand, tri, shift)


def _sorted_rows(tm, n_experts):
    return MOE_TOP_K * tm + n_experts * RUN_ROWS


def _proj_router_kernel(n_act, n_groups, n_experts, *refs):
    x_ref = refs[0]
    act_refs = refs[1:1 + n_act]
    w_refs = refs[1 + n_act:1 + 2 * n_act]
    b_ref, g_ref, wr_ref, br_ref, tri_ref, ltri_ref = refs[1 + 2 * n_act:7 + 2 * n_act]
    xnew_ref, xs_ref, gates_ref, counts_ref = refs[7 + 2 * n_act:]
    tm = x_ref.shape[0]
    epg = n_experts // n_groups

    y = x_ref[...] + b_ref[...]
    for a_ref, w_ref in zip(act_refs, w_refs):
        y = y + _dot(a_ref[...], w_ref[...])
    xnew_ref[...] = y
    xn = _rms(y, g_ref[...]).astype(BF16)

    logits = _dot(xn, wr_ref[...]) + br_ref[...]
    rr = -(-(n_groups + n_experts) // SUBLANES) * SUBLANES
    lt = logits.T[:rr, :]
    row = lax.broadcasted_iota(I32, lt.shape, 0)
    big = jnp.int32(LANES)
    neg = -jnp.inf

    def first_argmax(v):
        m = jnp.max(v, axis=0, keepdims=True)
        return m, jnp.min(jnp.where(v == m, row, big), axis=0, keepdims=True)

    gl = jnp.where(row < n_groups, lt, neg)
    gmax, gsel = first_argmax(gl)
    g_w = 1.0 / jnp.sum(jnp.exp(gl - gmax), axis=0, keepdims=True)
    erow = row - n_groups
    in_group = (erow >= 0) & (erow < n_experts) & ((erow // epg) == gsel)
    el = jnp.where(in_group, lt, neg)
    top1, i1 = first_argmax(el)
    top2, i2 = first_argmax(jnp.where(row == i1, neg, el))
    e2 = jnp.exp(top2 - top1)
    w1 = 1.0 / (1.0 + e2)
    gate1 = g_w * w1
    gate2 = g_w * (e2 * w1)

    sel1 = row == i1
    sel2 = row == i2
    onehot = jnp.where(sel1, 1.0, jnp.where(sel2, 1.0, 0.0))
    onehot_b = jnp.concatenate([onehot, jnp.zeros((LANES - rr, tm), F32)], axis=0).astype(BF16)
    before = _dot(onehot_b, tri_ref[...])[:rr, :]
    counts_ref[...] = lax.dot_general(jnp.ones((SUBLANES, tm), BF16), onehot_b, (((1,), (1,)), ((), ())),
                                      preferred_element_type=F32)[0:1, :]
    cnt = jnp.sum(onehot, axis=1, keepdims=True)
    cnt_pad = jnp.floor((cnt + (RUN_ROWS - 1)) * (1.0 / RUN_ROWS)) * RUN_ROWS
    cnt_pad = jnp.concatenate([jnp.broadcast_to(cnt_pad, (rr, LANES)), jnp.zeros((LANES - rr, LANES), F32)], axis=0)
    run_start = _dot(ltri_ref[...], cnt_pad.astype(BF16))[:rr, 0:1]
    slot = run_start + before
    j1 = jnp.sum(jnp.where(sel1, slot, 0.0), axis=0, keepdims=True)
    j2 = jnp.sum(jnp.where(sel2, slot, 0.0), axis=0, keepdims=True)
    trow = lax.broadcasted_iota(I32, (SUBLANES, tm), 0)
    table = jnp.where(trow == 0, gate1, jnp.where(trow == 1, gate2,
                                                  jnp.where(trow == 2, j1, jnp.where(trow == 3, j2, 0.0))))
    gates_ref[...] = jnp.concatenate([table, jnp.zeros((LANES - SUBLANES, tm), F32)], axis=0).T

    j1r = j1.astype(I32)
    j2r = j2.astype(I32)
    ri = lax.broadcasted_iota(I32, (xs_ref.shape[0], tm), 0)
    perm = jnp.where(ri == j1r, 1.0, jnp.where(ri == j2r, 1.0, 0.0)).astype(BF16)
    xs_ref[...] = _pack_halves(_dot(perm, xn), is_bf16=True)


def _proj_router(x2, acts, ws, bias, g, wr, br, tri, ltri, n_groups, n_experts):
    t, d = x2.shape
    tm = ROW_TILE
    n_tiles = t // tm
    lr = _sorted_rows(tm, n_experts)
    row = lambda n: pl.BlockSpec((tm, n), lambda i: (i, 0))
    full = lambda a: pl.BlockSpec(a.shape, lambda i: (0,) * a.ndim)
    kern = functools.partial(_proj_router_kernel, len(acts), n_groups, n_experts)
    return pl.pallas_call(
        kern,
        grid=(n_tiles,),
        in_specs=[row(d)] + [row(a.shape[1]) for a in acts] + [full(a) for a in (*ws, bias, g, wr, br, tri, ltri)],
        out_specs=[row(d), pl.BlockSpec((lr, d // 2), lambda i: (i, 0)), row(LANES),
                   pl.BlockSpec((None, 1, LANES), lambda i: (i, 0, 0))],
        out_shape=[jax.ShapeDtypeStruct((t, d), F32), jax.ShapeDtypeStruct((n_tiles * lr, d // 2), U32),
                   jax.ShapeDtypeStruct((t, LANES), F32), jax.ShapeDtypeStruct((n_tiles, 1, LANES), F32)],
        compiler_params=_cparams("parallel"),
        name="proj_router",
    )(x2, *acts, *ws, bias, g, wr, br, tri, ltri)


def _group_copies(table_ref, base, n, src_ref, dst_ref, sem):
    for q in range(n):
        g = table_ref[base + q]
        pltpu.make_async_copy(src_ref.at[pl.ds(pl.multiple_of(g * RUN_ROWS, RUN_ROWS), RUN_ROWS)],
                              dst_ref.at[pl.ds(q * RUN_ROWS, RUN_ROWS)], sem).start(priority=q % 2)


def _experts_kernel(be_ref, nused_ref, src_ref, xs_ref, wgu_ref, wd_ref, y_ref, xbuf, sems, wgu_b, wd_b):
    i = pl.program_id(0)
    n_used = nused_ref[0]
    rows = y_ref.shape[0]
    gpb = rows // RUN_ROWS
    slot = i % 2

    def fetch(blk, s):
        _group_copies(src_ref, blk * gpb, gpb, xs_ref, xbuf.at[s], sems.at[s])

    def wait(s):
        pltpu.make_async_copy(xs_ref.at[pl.ds(0, rows)], xbuf.at[s], sems.at[s]).wait()

    @pl.when(i == 0)
    def _():
        fetch(0, 0)

    @pl.when((i == 0) | (be_ref[i] != be_ref[jnp.maximum(i - 1, 0)]))
    def _():
        wgu_b[...] = wgu_ref[...].astype(BF16)
        wd_b[...] = wd_ref[...].astype(BF16)

    @pl.when(i >= n_used)
    def _():
        y_ref[...] = jnp.zeros_like(y_ref)

    @pl.when(i == n_used)
    def _():
        wait(slot)

    @pl.when(i < n_used)
    def _():
        wait(slot)
        fetch(i + 1, 1 - slot)
        ff = wd_b.shape[0]
        x = _unpack_halves(xbuf[slot]).astype(BF16)
        gu = _dot(x, wgu_b[...])
        h = (_silu(gu[:, :ff]) * gu[:, ff:]).astype(BF16)
        y_ref[...] = _pack_halves(_dot(h, wd_b[...]))


def _experts(block_expert, n_used, src_groups, xs, wgu, wd, layer, n_blocks):
    w = xs.shape[1]
    d, ff2 = wgu.shape[2:]
    rows = EXPERT_ROWS
    grid_spec = pltpu.PrefetchScalarGridSpec(
        num_scalar_prefetch=3,
        grid=(n_blocks,),
        in_specs=[pl.BlockSpec(memory_space=pl.ANY),
                  pl.BlockSpec((None, None, d, ff2), lambda i, be, nu, sg: (layer, be[i], 0, 0)),
                  pl.BlockSpec((None, None, ff2 // 2, d), lambda i, be, nu, sg: (layer, be[i], 0, 0))],
        out_specs=pl.BlockSpec((rows, w), lambda i, be, nu, sg: (i, 0)),
        scratch_shapes=[pltpu.VMEM((2, rows, w), U32), pltpu.SemaphoreType.DMA((2,)),
                        pltpu.VMEM((d, ff2), BF16), pltpu.VMEM((ff2 // 2, d), BF16)],
    )
    return pl.pallas_call(
        _experts_kernel,
        grid_spec=grid_spec,
        out_shape=jax.ShapeDtypeStruct((n_blocks * rows, w), U32),
        compiler_params=_cparams("arbitrary"),
        name="moe_experts",
    )(block_expert, n_used, src_groups, xs, wgu, wd)


def _combine_kernel(wide_ref, nwide_ref, single_ref, nsingle_ref, x_ref, gates_ref, y_ref, o_ref, ybuf, sems):
    i = pl.program_id(0)
    n_tiles = pl.num_programs(0)
    tm = x_ref.shape[0]
    lr = ybuf.shape[1]
    max_wide = wide_ref.shape[0] // nwide_ref.shape[0]
    max_single = single_ref.shape[0] // nsingle_ref.shape[0]
    slot = i % 2

    def fetch(tile, s):
        def copies(tab_ref, base, n, groups, priority):
            def start(p, carry):
                code = tab_ref[base + p]
                src = pl.multiple_of(lax.shift_right_logical(code, COPY_DST_BITS) * RUN_ROWS, RUN_ROWS)
                dst = pl.multiple_of((code & ((1 << COPY_DST_BITS) - 1)) * RUN_ROWS, RUN_ROWS)
                pltpu.make_async_copy(y_ref.at[pl.ds(src, groups * RUN_ROWS)],
                                      ybuf.at[s, pl.ds(dst, groups * RUN_ROWS)], sems.at[s]).start(priority=priority)
                return carry
            lax.fori_loop(0, n, start, 0)
        copies(wide_ref, tile * max_wide, nwide_ref[tile], COPY_GROUPS, 0)
        copies(single_ref, tile * max_single, nsingle_ref[tile], 1, 1)

    def wait(s):
        pltpu.make_async_copy(y_ref.at[pl.ds(0, lr)], ybuf.at[s], sems.at[s]).wait()

    @pl.when(i == 0)
    def _():
        fetch(0, 0)

    wait(slot)
    fetch(jnp.minimum(i + 1, n_tiles - 1), 1 - slot)
    ys = _unpack_halves(ybuf[slot]).astype(BF16)
    table = gates_ref[...]
    col = lax.broadcasted_iota(I32, (tm, lr), 1)
    pick = jnp.zeros((tm, lr), F32)
    for k in range(MOE_TOP_K):
        jk = table[:, MOE_TOP_K + k:MOE_TOP_K + k + 1].astype(I32)
        pick = jnp.where(col == jk, table[:, k:k + 1], pick)
    o_ref[...] = x_ref[...] + _dot(pick.astype(BF16), ys)

    @pl.when(i == n_tiles - 1)
    def _():
        wait(1 - slot)


def _combine(copy_lists, xnew, gates, y, lr):
    t, d = xnew.shape
    tm = ROW_TILE
    grid_spec = pltpu.PrefetchScalarGridSpec(
        num_scalar_prefetch=len(copy_lists),
        grid=(t // tm,),
        in_specs=[pl.BlockSpec((tm, d), lambda i, *p: (i, 0)),
                  pl.BlockSpec((tm, LANES), lambda i, *p: (i, 0)),
                  pl.BlockSpec(memory_space=pl.ANY)],
        out_specs=pl.BlockSpec((tm, d), lambda i, *p: (i, 0)),
        scratch_shapes=[pltpu.VMEM((2, lr, y.shape[1]), U32), pltpu.SemaphoreType.DMA((2,))],
    )
    return pl.pallas_call(
        _combine_kernel,
        grid_spec=grid_spec,
        out_shape=jax.ShapeDtypeStruct((t, d), F32),
        compiler_params=_cparams("arbitrary"),
        name="moe_combine",
    )(*copy_lists, xnew, gates, y)


def _moe_plan(counts, n_groups, n_experts, lr):
    n_tiles = counts.shape[0]
    gpb = EXPERT_ROWS // RUN_ROWS
    gpt = lr // RUN_ROWS
    cnt = counts[:, 0, n_groups:n_groups + n_experts].astype(I32)
    run = (cnt + RUN_ROWS - 1) // RUN_ROWS
    run_end = jnp.cumsum(run, axis=1)
    run_off = run_end - run
    used = run_end[:, -1]
    seg = jnp.sum(run, axis=0)
    seg_pad = (seg + gpb - 1) // gpb * gpb
    seg_end = jnp.cumsum(seg_pad)
    seg_start = seg_end - seg_pad
    before = jnp.cumsum(run, axis=0) - run
    n_blocks = (n_tiles * gpt + n_experts * (gpb - 1)) // gpb + 1
    n_used = (seg_end[-1] // gpb).astype(I32)
    blk = jnp.arange(n_blocks, dtype=I32)
    first = jnp.minimum(blk, n_used - 1) * gpb
    block_expert = jnp.minimum(jnp.sum(seg_end[None, :] <= first[:, None], axis=1), n_experts - 1).astype(I32)

    assert gpt < (1 << COPY_DST_BITS) and gpt - MOE_TOP_K * ROW_TILE // RUN_ROWS <= gpb
    length = jnp.concatenate([run, (gpt - used)[:, None]], axis=1)
    local = jnp.concatenate([run_off, used[:, None]], axis=1)
    glob = jnp.concatenate([seg_start[None, :] + before, jnp.full((n_tiles, 1), (n_blocks - 1) * gpb, I32)], axis=1)
    n_runs = length.shape[1]
    wide = length // COPY_GROUPS

    def copy_list(per_run, first, n_slots, step):
        end = jnp.cumsum(per_run, axis=1)
        p = jnp.arange(n_slots, dtype=I32)
        owner = jnp.minimum(jnp.sum(end[:, None, :] <= p[None, :, None], axis=2), n_runs - 1)
        own = (owner[:, :, None] == jnp.arange(n_runs, dtype=I32)[None, None, :]).astype(I32)
        sel = lambda tab: jnp.sum(own * tab[:, None, :], axis=2)
        off = sel(first) + (p[None, :] - sel(end - per_run)) * step
        code = ((sel(glob) + off) << COPY_DST_BITS) + sel(local) + off
        return code.astype(I32).reshape(-1), end[:, -1].astype(I32)

    wide_list = copy_list(wide, jnp.zeros_like(wide), gpt // COPY_GROUPS, COPY_GROUPS)
    single_list = copy_list(length - wide * COPY_GROUPS, wide * COPY_GROUPS, n_runs * (COPY_GROUPS - 1), 1)
    copy_lists = (*wide_list, *single_list)

    pick = lambda onehot, tab: jnp.dot(onehot, tab.astype(F32), precision=lax.Precision.HIGHEST)
    zero_src = gpt - 1
    g = jnp.arange(n_blocks * gpb, dtype=I32)
    ge = jnp.minimum(jnp.sum(seg_end[None, :] <= g[:, None], axis=1), n_experts - 1)
    oh_e = (ge[:, None] == jnp.arange(n_experts, dtype=I32)[None, :]).astype(F32)
    per_e = pick(oh_e, jnp.stack([seg_start, seg], axis=1))
    m = g.astype(F32) - per_e[:, 0]
    tile_end = pick(oh_e, jnp.cumsum(run, axis=0).T)
    tile = jnp.minimum(jnp.sum(tile_end <= m[:, None], axis=1), n_tiles - 1)
    oh_t = (tile[:, None] == jnp.arange(n_tiles, dtype=I32)[None, :]).astype(F32)
    tile_base = (jnp.arange(n_tiles, dtype=I32)[:, None] * gpt + run_off - before).T
    src = jnp.sum(oh_t * pick(oh_e, tile_base), axis=1) + m
    src = jnp.where(m < per_e[:, 1], src, float(zero_src)).astype(I32)
    return block_expert, n_used.reshape(1), src, copy_lists, n_blocks


def _moe(xnew, xs, gates, counts, wgu, wd, layer, n_groups):
    n_experts = wgu.shape[1]
    lr = _sorted_rows(ROW_TILE, n_experts)
    block_expert, n_used, src, copy_lists, n_blocks = _moe_plan(counts, n_groups, n_experts, lr)
    y = _experts(block_expert, n_used, src, xs, wgu, wd, layer, n_blocks)
    return _combine(copy_lists, xnew, gates, y, lr)


def _pw1_kernel(x_ref, g_ref, w_ref, b_ref, u_ref):
    ch = u_ref.shape[1]
    xn = _rms(x_ref[...], g_ref[...]).astype(BF16)
    y = _dot(xn, w_ref[...]) + b_ref[...]
    u_ref[...] = (y[:, :ch] * jax.nn.sigmoid(y[:, ch:])).astype(BF16)


def _pw1(x2, g, w, b):
    t, d = x2.shape
    tm = ROW_TILE
    ch = w.shape[1] // 2
    full = lambda a: pl.BlockSpec(a.shape, lambda i: (0,) * a.ndim)
    return pl.pallas_call(
        _pw1_kernel,
        grid=(t // tm,),
        in_specs=[pl.BlockSpec((tm, d), lambda i: (i, 0)), full(g), full(w), full(b)],
        out_specs=pl.BlockSpec((tm, ch), lambda i: (i, 0)),
        out_shape=jax.ShapeDtypeStruct((t, ch), BF16),
        compiler_params=_cparams("parallel"),
        name="pw1_glu",
    )(x2, g, w, b)


def _dwconv_kernel(cur_ref, halo_ref, w_ref, b_ref, lg_ref, lb_ref, o_ref, xs_ref, acc_ref):
    ts, ch = cur_ref.shape
    n_strip, n_tap = w_ref.shape[:2]
    n_chunk = ts // CONV_ROWS
    first_tap = CONV_HALO - (n_tap - 1)
    halo = halo_ref[...].astype(F32)
    halo = jnp.where(pl.program_id(1) > 0, halo, jnp.zeros_like(halo))
    cur = cur_ref[...].astype(F32)
    shifted_rows = ts + CONV_HALO - SUBLANES
    for c in range(n_strip):
        cs = slice(c * LANES, (c + 1) * LANES)
        xs_ref[0, c, :CONV_HALO, :] = halo[:, cs]
        xs_ref[0, c, CONV_HALO:, :] = cur[:, cs]
        for s in range(1, SUBLANES):
            xs_ref[s, c, :shifted_rows, :] = xs_ref[0, c, s:s + shifted_rows, :]

    def strip_chunk(idx, carry):
        r = idx // n_strip
        c = idx % n_strip
        row0 = pl.multiple_of(r * CONV_ROWS, CONV_ROWS)
        acc = jnp.broadcast_to(b_ref[c], (CONV_ROWS, LANES))
        for s in range(SUBLANES):
            taps = [k for k in range(n_tap) if (first_tap + k) % SUBLANES == s]
            lo = (first_tap + taps[0]) // SUBLANES * SUBLANES
            hi = (first_tap + taps[-1]) // SUBLANES * SUBLANES
            data = xs_ref[s, c, pl.ds(row0 + lo, hi - lo + CONV_ROWS), :]
            for k in taps:
                a = (first_tap + k) // SUBLANES * SUBLANES - lo
                acc = acc + w_ref[c, k:k + 1, :] * data[a:a + CONV_ROWS, :]
        acc_ref[c, pl.ds(row0, CONV_ROWS), :] = acc
        return carry

    lax.fori_loop(0, n_chunk * n_strip, strip_chunk, 0, unroll=2)
    y = jnp.concatenate([acc_ref[c] for c in range(n_strip)], axis=1)
    mu = jnp.mean(y, axis=-1, keepdims=True)
    cen = y - mu
    var = jnp.mean(cen * cen, axis=-1, keepdims=True)
    v = cen * lax.rsqrt(var + LN_EPS) * lg_ref[...] + lb_ref[...]
    o_ref[...] = _silu(v).astype(BF16)


def _dwconv(u, w, b, lg, lb, bsz, seq):
    ts = CONV_TILE
    nt = seq // ts
    ch = u.shape[1]
    hpt = ts // CONV_HALO
    n_strip = ch // LANES
    full = lambda a: pl.BlockSpec(a.shape, lambda bb, i: (0,) * a.ndim)
    w = w.reshape(-1, n_strip, LANES).transpose(1, 0, 2)
    b = b.reshape(n_strip, 1, LANES)
    return pl.pallas_call(
        _dwconv_kernel,
        grid=(bsz, nt),
        in_specs=[pl.BlockSpec((ts, ch), lambda bb, i: (bb * nt + i, 0)),
                  pl.BlockSpec((CONV_HALO, ch), lambda bb, i: (jnp.maximum((bb * nt + i) * hpt - 1, 0), 0)),
                  full(w), full(b), full(lg), full(lb)],
        out_specs=pl.BlockSpec((ts, ch), lambda bb, i: (bb * nt + i, 0)),
        out_shape=jax.ShapeDtypeStruct(u.shape, BF16),
        scratch_shapes=[pltpu.VMEM((SUBLANES, n_strip, CONV_HALO + ts, LANES), F32),
                        pltpu.VMEM((n_strip, ts, LANES), F32)],
        compiler_params=_cparams("parallel", "parallel"),
        name="dwconv_ln",
    )(u, u, w, b, lg, lb)


def _row(v):
    return v.reshape(1, -1).astype(F32)


def _pad_lanes(v, n=LANES):
    v = v.reshape(1, -1).astype(F32)
    return jnp.pad(v, ((0, 0), (0, n - v.shape[1])))


def _router_params(w_group, b_group, w_expert, b_expert):
    wr = jnp.concatenate([w_group, w_expert], axis=1)
    wr = jnp.pad(wr, ((0, 0), (0, LANES - wr.shape[1]))).astype(BF16)
    br = _pad_lanes(jnp.concatenate([b_group.reshape(-1), b_expert.reshape(-1)]))
    return wr, br


def kernel(x, mix_norm_g, w_in, q_norm_g, k_norm_g, attn_sinks, ssm_conv_w, ssm_conv_b, ssm_dt_bias, ssm_a_log, ssm_d, ssm_out_norm_g, w_out, conv_norm_g, conv_w_pw1, conv_b_pw1, conv_w_dw, conv_b_dw, conv_ln_g, conv_ln_b, conv_w_pw2, conv_b_pw2, moe_norm_g, moe_w_group, moe_b_group, moe_w_expert, moe_b_expert, moe_w_gate_up, moe_w_down):
    bsz, seq, d = x.shape
    t = bsz * seq
    n_heads = attn_sinks.shape[1]
    n_kv = n_heads // GQA_REP
    q_cols = n_heads * HEAD_DIM
    kv_cols = n_kv * HEAD_DIM
    d_inner = ssm_out_norm_g.shape[1]
    ssm_heads = ssm_a_log.shape[1]
    n_groups = moe_w_group.shape[2]
    n_experts = moe_w_expert.shape[2]
    assert t % ROW_TILE == 0 and seq % CONV_TILE == 0 and seq % (SSM_CHUNK * SSD_STEP_CHUNKS) == 0 and seq % (ATTN_BLOCK * ATTN_STEP_BLOCKS) == 0
    assert ssm_heads <= LANES and n_groups + n_experts <= LANES

    x2 = x.reshape(t, d)
    tri_strict = jnp.asarray(np.triu(np.ones((ROW_TILE, ROW_TILE), np.float32), 1), BF16)
    lane_before = jnp.asarray(np.tril(np.ones((LANES, LANES), np.float32), -1), BF16)
    zero_bias = jnp.zeros((1, d), F32)

    w = w_in[0]
    cuts = [0] + np.cumsum([q_cols, kv_cols, kv_cols, d_inner, ssm_conv_w.shape[2]]).tolist() + [w.shape[1]]
    wq, wk, wv, wz, wx, wdt = (w[:, lo:hi] for lo, hi in zip(cuts[:-1], cuts[1:]))
    wdt = jnp.pad(wdt, ((0, 0), (0, LANES - ssm_heads)))
    seg_ones = jnp.asarray(np.kron(np.eye(q_cols // HEAD_DIM, dtype=np.float32),
                                   np.ones((HEAD_DIM, HEAD_DIM), np.float32)), BF16)
    qg = jnp.tile(_row(q_norm_g[0]), (1, n_heads))
    kg = jnp.tile(_row(k_norm_g[0]), (1, n_kv))
    q, k, v, z, xbc, dt = _inproj(
        x2, _row(mix_norm_g[0]), wq.astype(BF16), wk.astype(BF16), wv.astype(BF16), wz.astype(BF16),
        wx.astype(BF16), wdt.astype(BF16), qg, kg, seg_ones)
    kv_rep = jnp.asarray(np.stack([np.kron(np.eye(n_kv, dtype=np.float32)[:, g:g + 1],
                                           np.tile(np.eye(HEAD_DIM, dtype=np.float32), (1, GQA_REP)))
                                   for g in range(n_kv)]), BF16)
    y_attn = _attention(attn_sinks[0].reshape(1, n_heads).astype(F32), kv_rep, q, k, v, bsz, seq)
    expand = jnp.asarray(np.kron(np.eye(LANES, ssm_heads, dtype=np.float32),
                                 np.ones((1, SSM_HEAD_DIM), np.float32)), BF16)
    tri_incl = jnp.asarray(np.tril(np.ones((SSM_CHUNK, SSM_CHUNK), np.float32)), BF16)
    n_tap = ssm_conv_w.shape[1]
    shift = jnp.asarray(np.concatenate([np.eye(SSM_CHUNK, 2 * SSM_CHUNK, SSM_CHUNK - j, dtype=np.float32)
                                        for j in range(1, n_tap)], axis=0), BF16)
    y_ssm = _ssd(xbc, z, dt, ssm_conv_w[0].astype(F32), _row(ssm_conv_b[0]), _pad_lanes(ssm_dt_bias[0]),
                 _pad_lanes(ssm_a_log[0]), jnp.repeat(_row(ssm_d[0]), SSM_HEAD_DIM, axis=1), _row(ssm_out_norm_g[0]),
                 expand, tri_incl, shift, bsz, seq)
    wo = w_out[0].astype(BF16)
    wr, br = _router_params(moe_w_group[0], moe_b_group[0], moe_w_expert[0], moe_b_expert[0])
    xnew, xs, gates, counts = _proj_router(
        x2, [y_attn, y_ssm], [wo[:q_cols], wo[q_cols:]], zero_bias, _row(moe_norm_g[0]), wr, br, tri_strict,
        lane_before, n_groups, n_experts)
    x2 = _moe(xnew, xs, gates, counts, moe_w_gate_up, moe_w_down, 0, n_groups)

    u = _pw1(x2, _row(conv_norm_g[0]), conv_w_pw1[0].astype(BF16), _row(conv_b_pw1[0]))
    u = _dwconv(u, conv_w_dw[0].astype(F32), _row(conv_b_dw[0]), _row(conv_ln_g[0]), _row(conv_ln_b[0]), bsz, seq)
    wr, br = _router_params(moe_w_group[1], moe_b_group[1], moe_w_expert[1], moe_b_expert[1])
    xnew, xs, gates, counts = _proj_router(
        x2, [u], [conv_w_pw2[0].astype(BF16)], _row(conv_b_pw2[0]), _row(moe_norm_g[1]), wr, br, tri_strict,
        lane_before, n_groups, n_experts)
    x2 = _moe(xnew, xs, gates, counts, moe_w_gate_up, moe_w_down, 1, n_groups)
    return x2.reshape(bsz, seq, d)
```

```python
import functools

import numpy as np
import jax
import jax.numpy as jnp
from jax import lax
from jax.experimental import pallas as pl
from jax.experimental.pallas import tpu as pltpu

F32 = jnp.float32
BF16 = jnp.bfloat16
I32 = jnp.int32
U32 = jnp.uint32

HEAD_DIM = 64
GQA_REP = 4
ATTN_BLOCK = 128
SSM_HEAD_DIM = 64
SSM_GROUPS = 2
SSM_D_STATE = 128
SSM_CHUNK = 128
MOE_TOP_K = 2
RMS_EPS = 1e-6
LN_EPS = 1e-5

LANES = 128
SUBLANES = 8

ROW_TILE = 512
EXPERT_ROWS = 512
SSD_STEP_CHUNKS = 8
ATTN_STEP_BLOCKS = 8
ATTN_HEADS_PER_DOT = 4
CONV_TILE = 512
CONV_HALO = 32
CONV_ROWS = 64
RUN_ROWS = SUBLANES
COPY_GROUPS = 4
COPY_DST_BITS = 8
VMEM_LIMIT = 48 * 1024 * 1024


def _cparams(*sem):
    return pltpu.CompilerParams(dimension_semantics=sem, vmem_limit_bytes=VMEM_LIMIT)


def _dot(a, b):
    return jnp.dot(a, b, preferred_element_type=F32)


def _rms(x, g):
    return x * lax.rsqrt(jnp.mean(x * x, axis=-1, keepdims=True) + RMS_EPS) * g


def _silu(x):
    return x * jax.nn.sigmoid(x)


def _pack_halves(y, is_bf16=False):
    c = y.shape[1] // 2
    if is_bf16:
        return lax.bitcast_convert_type(y[:, :c], U32) | (lax.bitcast_convert_type(y[:, c:], U32) >> 16)
    hi = lax.bitcast_convert_type(y[:, :c].astype(BF16).astype(F32), U32)
    lo = lax.bitcast_convert_type(y[:, c:].astype(BF16).astype(F32), U32)
    return (hi & jnp.uint32(0xFFFF0000)) | (lo >> 16)


def _unpack_halves(u):
    hi = lax.bitcast_convert_type(u & jnp.uint32(0xFFFF0000), F32)
    lo = lax.bitcast_convert_type(u << 16, F32)
    return jnp.concatenate([hi, lo], axis=1)


def _inproj_kernel(x_ref, g_ref, wq_ref, wk_ref, wv_ref, wz_ref, wx_ref, wdt_ref, qg_ref, kg_ref, bd_ref,
                   q_out, k_out, v_out, z_out, xbc_out, dt_out):
    xn = _rms(x_ref[...], g_ref[...]).astype(BF16)

    def head_rms(y, gain):
        bd = bd_ref[:y.shape[1], :y.shape[1]]
        sq = y * y
        hi = sq.astype(BF16)
        lo = (sq - hi.astype(F32)).astype(BF16)
        ss = _dot(hi, bd) + _dot(lo, bd)
        return y * lax.rsqrt(ss * (1.0 / HEAD_DIM) + RMS_EPS) * gain

    q = head_rms(_dot(xn, wq_ref[...]), qg_ref[...])
    q_out[...] = (q * (HEAD_DIM ** -0.5)).astype(BF16)
    k_out[...] = head_rms(_dot(xn, wk_ref[...]), kg_ref[...]).astype(BF16)
    v_out[...] = _dot(xn, wv_ref[...]).astype(BF16)
    z_out[...] = _dot(xn, wz_ref[...]).astype(BF16)
    xbc_out[...] = _dot(xn, wx_ref[...]).astype(BF16)
    dt_out[...] = _dot(xn, wdt_ref[...])


def _inproj(x2, g, wq, wk, wv, wz, wx, wdt, qg, kg, bd):
    t, d = x2.shape
    tm = ROW_TILE
    row = lambda n: pl.BlockSpec((tm, n), lambda i: (i, 0))
    full = lambda a: pl.BlockSpec(a.shape, lambda i: (0,) * a.ndim)
    outs = [(wq.shape[1], BF16), (wk.shape[1], BF16), (wv.shape[1], BF16), (wz.shape[1], BF16),
            (wx.shape[1], BF16), (wdt.shape[1], F32)]
    return pl.pallas_call(
        _inproj_kernel,
        grid=(t // tm,),
        in_specs=[row(d)] + [full(a) for a in (g, wq, wk, wv, wz, wx, wdt, qg, kg, bd)],
        out_specs=[row(n) for n, _ in outs],
        out_shape=[jax.ShapeDtypeStruct((t, n), dt) for n, dt in outs],
        compiler_params=_cparams("parallel"),
        name="inproj",
    )(x2, g, wq, wk, wv, wz, wx, wdt, qg, kg, bd)


def _attn_kernel(sink_ref, rep_ref, q_ref, kc_ref, kp_ref, vc_ref, vp_ref, o_ref):
    n = pl.program_id(1)
    blk = ATTN_BLOCK
    gw = GQA_REP * HEAD_DIM
    n_kv = q_ref.shape[1] // gw
    qi = lax.broadcasted_iota(I32, (blk, 2 * blk), 0)
    kj = lax.broadcasted_iota(I32, (blk, 2 * blk), 1)
    rel = qi + blk - kj
    in_window = (rel >= 0) & (rel < ATTN_BLOCK)
    first_key = jnp.where(n > 0, 0, blk)
    key_head = lax.broadcasted_iota(I32, (2 * blk, gw), 1) // HEAD_DIM
    out_head = lax.broadcasted_iota(I32, (blk, gw), 1) // HEAD_DIM
    for sb, g in [(sb, g) for sb in range(ATTN_STEP_BLOCKS) for g in range(n_kv)]:
        sl = slice(g * gw, (g + 1) * gw)
        rows = slice(sb * blk, (sb + 1) * blk)
        before = slice((sb - 1) * blk, sb * blk)
        band = in_window & (kj >= first_key) if sb == 0 else in_window
        qg = q_ref[rows, sl]
        kcat = jnp.concatenate([kp_ref[...] if sb == 0 else kc_ref[before, :], kc_ref[rows, :]], axis=0)
        vcat = jnp.concatenate([vp_ref[...] if sb == 0 else vc_ref[before, :], vc_ref[rows, :]], axis=0)
        kk = _dot(kcat, rep_ref[g]).astype(BF16)
        vv = _dot(vcat, rep_ref[g]).astype(BF16)
        zero = jnp.zeros_like(kk)
        o, scale = None, None
        for heads in [range(h0, h0 + ATTN_HEADS_PER_DOT) for h0 in range(0, GQA_REP, ATTN_HEADS_PER_DOT)]:
            kbd = jnp.concatenate([jnp.where(key_head == h, kk, zero) for h in heads], axis=0)
            vbd = jnp.concatenate([jnp.where(key_head == h, vv, zero) for h in heads], axis=0)
            s = lax.dot_general(qg, kbd, (((1,), (1,)), ((), ())), preferred_element_type=F32)
            probs = []
            for n_h, h in enumerate(heads):
                sh = jnp.where(band, s[:, n_h * 2 * blk:(n_h + 1) * 2 * blk], -jnp.inf)
                sink = sink_ref[0, g * GQA_REP + h]
                m = jnp.maximum(jnp.max(sh, axis=-1, keepdims=True), sink)
                p = jnp.exp(sh - m)
                denom = jnp.sum(p, axis=-1, keepdims=True) + jnp.exp(sink - m)
                probs.append(p.astype(BF16))
                inv = jnp.broadcast_to(1.0 / denom, (blk, gw))
                scale = inv if scale is None else jnp.where(out_head == h, inv, scale)
            part = _dot(jnp.concatenate(probs, axis=1), vbd)
            o = part if o is None else o + part
        o_ref[rows, sl] = (o * scale).astype(BF16)


def _attention(sinks, kv_rep, q, k, v, bsz, seq):
    blk = ATTN_BLOCK
    spb = ATTN_STEP_BLOCKS
    nb = seq // (blk * spb)
    cur = lambda w: pl.BlockSpec((blk * spb, w), lambda b, n: (b * nb + n, 0))
    prev = lambda w: pl.BlockSpec((blk, w), lambda b, n: (jnp.maximum((b * nb + n) * spb - 1, 0), 0))
    qw, kw = q.shape[1], k.shape[1]
    return pl.pallas_call(
        _attn_kernel,
        grid=(bsz, nb),
        in_specs=[pl.BlockSpec(memory_space=pltpu.SMEM), pl.BlockSpec(kv_rep.shape, lambda b, n: (0, 0, 0)),
                  cur(qw), cur(kw), prev(kw), cur(kw), prev(kw)],
        out_specs=cur(qw),
        out_shape=jax.ShapeDtypeStruct(q.shape, BF16),
        compiler_params=_cparams("parallel", "parallel"),
        name="swa_attention",
    )(sinks, kv_rep, q, k, k, v, v)


def _split3(a):
    a1 = a.astype(BF16)
    r = a - a1.astype(F32)
    a2 = r.astype(BF16)
    return a1, a2, (r - a2.astype(F32)).astype(BF16)


def _ssd_kernel(xbc_ref, z_ref, dt_ref, cw_ref, cb_ref, dtb_ref, alog_ref, dskip_ref, og_ref, expand_ref, tri_ref,
                shift_ref, y_ref, prev_ref, state_ref):
    lc = SSM_CHUNK

    @pl.when(pl.program_id(1) == 0)
    def _():
        prev_ref[...] = jnp.zeros_like(prev_ref)
        state_ref[...] = jnp.zeros_like(state_ref)

    for cc in range(SSD_STEP_CHUNKS):
        rows = slice(cc * lc, (cc + 1) * lc)
        prev = prev_ref[...] if cc == 0 else xbc_ref[(cc - 1) * lc:cc * lc, :]
        _ssd_chunk(xbc_ref[rows, :], prev, z_ref[rows, :], dt_ref[rows, :], cw_ref, cb_ref, dtb_ref, alog_ref,
                   dskip_ref, og_ref, expand_ref, tri_ref, shift_ref, y_ref.at[rows, :], state_ref)
    prev_ref[...] = xbc_ref[(SSD_STEP_CHUNKS - 1) * lc:, :]


def _ssd_chunk(xb, prev, z, dt, cw_ref, cb_ref, dtb_ref, alog_ref, dskip_ref, og_ref, expand_ref, tri_ref, shift_ref,
               y_ref, state_ref):
    lc = xb.shape[0]
    d_inner = z.shape[1]
    n_heads = d_inner // SSM_HEAD_DIM
    gn = SSM_GROUPS * SSM_D_STATE
    hpg = n_heads // SSM_GROUPS
    gw = hpg * SSM_HEAD_DIM
    n_tap = cw_ref.shape[0]

    sh = _dot(shift_ref[...], jnp.concatenate([prev, xb], axis=0))
    acc = cb_ref[...] + cw_ref[n_tap - 1:n_tap, :] * xb.astype(F32)
    for j in range(1, n_tap):
        acc = acc + cw_ref[n_tap - 1 - j:n_tap - j, :] * sh[(j - 1) * lc:j * lc, :]
    u = _silu(acc)
    xs = u[:, :d_inner]
    bm = u[:, d_inner:d_inner + gn].astype(BF16)
    cm = u[:, d_inner + gn:].astype(BF16)

    dtr = dt + dtb_ref[...]
    dt_c = jnp.maximum(dtr, 0.0) + jnp.log1p(jnp.exp(-jnp.abs(dtr)))
    a_c = dt_c * (-jnp.exp(alog_ref[...]))
    cum = _dot(tri_ref[...], jnp.concatenate(_split3(a_c), axis=1))
    acum_c = (cum[:, :LANES] + cum[:, LANES:2 * LANES]) + cum[:, 2 * LANES:]
    acum_r = acum_c.T
    ex = _dot(jnp.concatenate(_split3(dt_c) + _split3(acum_c), axis=0), expand_ref[...])
    dt_e = (ex[:lc] + ex[lc:2 * lc]) + ex[2 * lc:3 * lc]
    acum_e = (ex[3 * lc:4 * lc] + ex[4 * lc:5 * lc]) + ex[5 * lc:]
    xdt = xs * dt_e

    li = lax.broadcasted_iota(I32, (lc, lc), 0)
    si = lax.broadcasted_iota(I32, (lc, lc), 1)
    causal = li >= si
    mats = []
    for g in range(SSM_GROUPS):
        cb = lax.dot_general(cm[:, g * SSM_D_STATE:(g + 1) * SSM_D_STATE], bm[:, g * SSM_D_STATE:(g + 1) * SSM_D_STATE],
                             (((1,), (1,)), ((), ())), preferred_element_type=F32)
        for r in range(hpg):
            h = g * hpg + r
            seg = acum_c[:, h:h + 1] - acum_r[h:h + 1, :]
            decay = jnp.exp(jnp.where(causal, seg, -jnp.inf))
            mats.append((cb * decay).astype(BF16))
    xdt_b = xdt.astype(BF16)
    row_head = lax.broadcasted_iota(I32, (lc, d_inner), 1) // SSM_HEAD_DIM
    zero = jnp.zeros_like(xdt_b)
    xbd = jnp.concatenate([jnp.where(row_head == h, xdt_b, zero) for h in range(n_heads)], axis=0)
    y = _dot(jnp.concatenate(mats, axis=1), xbd)

    a_last = acum_e[lc - 1:lc, :]
    w_state = (xdt * jnp.exp(a_last - acum_e)).astype(BF16)
    y_off = []
    for g in range(SSM_GROUPS):
        ns = slice(g * SSM_D_STATE, (g + 1) * SSM_D_STATE)
        hs = slice(g * gw, (g + 1) * gw)
        h_in = state_ref[:, hs]
        y_off.append(_dot(cm[:, ns], h_in.astype(BF16)))
        new = lax.dot_general(bm[:, ns], w_state[:, hs], (((0,), (0,)), ((), ())), preferred_element_type=F32)
        state_ref[:, hs] = h_in * jnp.exp(a_last[:, hs]) + new
    y = y + jnp.concatenate(y_off, axis=1) * jnp.exp(acum_e)

    y = y + dskip_ref[...] * xs
    y = y * _silu(z.astype(F32))
    y_ref[...] = _rms(y, og_ref[...]).astype(BF16)


def _ssd(xbc, z, dt, cw, cb, dtb, alog, dskip, og, expand, tri, shift, bsz, seq):
    rows = SSM_CHUNK * SSD_STEP_CHUNKS
    ns = seq // rows
    blk = lambda n: pl.BlockSpec((rows, n), lambda b, c: (b * ns + c, 0))
    full = lambda a: pl.BlockSpec(a.shape, lambda b, c: (0,) * a.ndim)
    d_inner = z.shape[1]
    return pl.pallas_call(
        _ssd_kernel,
        grid=(bsz, ns),
        in_specs=[blk(xbc.shape[1]), blk(d_inner), blk(dt.shape[1])]
        + [full(a) for a in (cw, cb, dtb, alog, dskip, og, expand, tri, shift)],
        out_specs=blk(d_inner),
        out_shape=jax.ShapeDtypeStruct(z.shape, BF16),
        scratch_shapes=[pltpu.VMEM((SSM_CHUNK, xbc.shape[1]), BF16), pltpu.VMEM((SSM_D_STATE, d_inner), F32)],
        compiler_params=_cparams("arbitrary", "arbitrary"),
        name="ssd_scan",
    )(xbc, z, dt, cw, cb, dtb, alog, dskip, og, expand, tri, shift)


def _sorted_rows(tm, n_experts):
    return MOE_TOP_K * tm + n_experts * RUN_ROWS


def _proj_router_kernel(n_act, n_groups, n_experts, *refs):
    x_ref = refs[0]
    act_refs = refs[1:1 + n_act]
    w_refs = refs[1 + n_act:1 + 2 * n_act]
    b_ref, g_ref, wr_ref, br_ref, tri_ref, ltri_ref = refs[1 + 2 * n_act:7 + 2 * n_act]
    xnew_ref, xs_ref, gates_ref, counts_ref = refs[7 + 2 * n_act:]
    tm = x_ref.shape[0]
    epg = n_experts // n_groups

    y = x_ref[...] + b_ref[...]
    for a_ref, w_ref in zip(act_refs, w_refs):
        y = y + _dot(a_ref[...], w_ref[...])
    xnew_ref[...] = y
    xn = _rms(y, g_ref[...]).astype(BF16)

    logits = _dot(xn, wr_ref[...]) + br_ref[...]
    rr = -(-(n_groups + n_experts) // SUBLANES) * SUBLANES
    lt = logits.T[:rr, :]
    row = lax.broadcasted_iota(I32, lt.shape, 0)
    big = jnp.int32(LANES)
    neg = -jnp.inf

    def first_argmax(v):
        m = jnp.max(v, axis=0, keepdims=True)
        return m, jnp.min(jnp.where(v == m, row, big), axis=0, keepdims=True)

    gl = jnp.where(row < n_groups, lt, neg)
    gmax, gsel = first_argmax(gl)
    g_w = 1.0 / jnp.sum(jnp.exp(gl - gmax), axis=0, keepdims=True)
    erow = row - n_groups
    in_group = (erow >= 0) & (erow < n_experts) & ((erow // epg) == gsel)
    el = jnp.where(in_group, lt, neg)
    top1, i1 = first_argmax(el)
    top2, i2 = first_argmax(jnp.where(row == i1, neg, el))
    e2 = jnp.exp(top2 - top1)
    w1 = 1.0 / (1.0 + e2)
    gate1 = g_w * w1
    gate2 = g_w * (e2 * w1)

    sel1 = row == i1
    sel2 = row == i2
    onehot = jnp.where(sel1, 1.0, jnp.where(sel2, 1.0, 0.0))
    onehot_b = jnp.concatenate([onehot, jnp.zeros((LANES - rr, tm), F32)], axis=0).astype(BF16)
    before = _dot(onehot_b, tri_ref[...])[:rr, :]
    counts_ref[...] = lax.dot_general(jnp.ones((SUBLANES, tm), BF16), onehot_b, (((1,), (1,)), ((), ())),
                                      preferred_element_type=F32)[0:1, :]
    cnt = jnp.sum(onehot, axis=1, keepdims=True)
    cnt_pad = jnp.floor((cnt + (RUN_ROWS - 1)) * (1.0 / RUN_ROWS)) * RUN_ROWS
    cnt_pad = jnp.concatenate([jnp.broadcast_to(cnt_pad, (rr, LANES)), jnp.zeros((LANES - rr, LANES), F32)], axis=0)
    run_start = _dot(ltri_ref[...], cnt_pad.astype(BF16))[:rr, 0:1]
    slot = run_start + before
    j1 = jnp.sum(jnp.where(sel1, slot, 0.0), axis=0, keepdims=True)
    j2 = jnp.sum(jnp.where(sel2, slot, 0.0), axis=0, keepdims=True)
    trow = lax.broadcasted_iota(I32, (SUBLANES, tm), 0)
    table = jnp.where(trow == 0, gate1, jnp.where(trow == 1, gate2,
                                                  jnp.where(trow == 2, j1, jnp.where(trow == 3, j2, 0.0))))
    gates_ref[...] = jnp.concatenate([table, jnp.zeros((LANES - SUBLANES, tm), F32)], axis=0).T

    j1r = j1.astype(I32)
    j2r = j2.astype(I32)
    ri = lax.broadcasted_iota(I32, (xs_ref.shape[0], tm), 0)
    perm = jnp.where(ri == j1r, 1.0, jnp.where(ri == j2r, 1.0, 0.0)).astype(BF16)
    xs_ref[...] = _pack_halves(_dot(perm, xn), is_bf16=True)


def _proj_router(x2, acts, ws, bias, g, wr, br, tri, ltri, n_groups, n_experts):
    t, d = x2.shape
    tm = ROW_TILE
    n_tiles = t // tm
    lr = _sorted_rows(tm, n_experts)
    row = lambda n: pl.BlockSpec((tm, n), lambda i: (i, 0))
    full = lambda a: pl.BlockSpec(a.shape, lambda i: (0,) * a.ndim)
    kern = functools.partial(_proj_router_kernel, len(acts), n_groups, n_experts)
    return pl.pallas_call(
        kern,
        grid=(n_tiles,),
        in_specs=[row(d)] + [row(a.shape[1]) for a in acts] + [full(a) for a in (*ws, bias, g, wr, br, tri, ltri)],
        out_specs=[row(d), pl.BlockSpec((lr, d // 2), lambda i: (i, 0)), row(LANES),
                   pl.BlockSpec((None, 1, LANES), lambda i: (i, 0, 0))],
        out_shape=[jax.ShapeDtypeStruct((t, d), F32), jax.ShapeDtypeStruct((n_tiles * lr, d // 2), U32),
                   jax.ShapeDtypeStruct((t, LANES), F32), jax.ShapeDtypeStruct((n_tiles, 1, LANES), F32)],
        compiler_params=_cparams("parallel"),
        name="proj_router",
    )(x2, *acts, *ws, bias, g, wr, br, tri, ltri)


def _group_copies(table_ref, base, n, src_ref, dst_ref, sem):
    for q in range(n):
        g = table_ref[base + q]
        pltpu.make_async_copy(src_ref.at[pl.ds(pl.multiple_of(g * RUN_ROWS, RUN_ROWS), RUN_ROWS)],
                              dst_ref.at[pl.ds(q * RUN_ROWS, RUN_ROWS)], sem).start(priority=q % 2)


def _experts_kernel(be_ref, nused_ref, src_ref, xs_ref, wgu_ref, wd_ref, y_ref, xbuf, sems, wgu_b, wd_b):
    i = pl.program_id(0)
    n_used = nused_ref[0]
    rows = y_ref.shape[0]
    gpb = rows // RUN_ROWS
    slot = i % 2

    def fetch(blk, s):
        _group_copies(src_ref, blk * gpb, gpb, xs_ref, xbuf.at[s], sems.at[s])

    def wait(s):
        pltpu.make_async_copy(xs_ref.at[pl.ds(0, rows)], xbuf.at[s], sems.at[s]).wait()

    @pl.when(i == 0)
    def _():
        fetch(0, 0)

    @pl.when((i == 0) | (be_ref[i] != be_ref[jnp.maximum(i - 1, 0)]))
    def _():
        wgu_b[...] = wgu_ref[...].astype(BF16)
        wd_b[...] = wd_ref[...].astype(BF16)

    @pl.when(i >= n_used)
    def _():
        y_ref[...] = jnp.zeros_like(y_ref)

    @pl.when(i == n_used)
    def _():
        wait(slot)

    @pl.when(i < n_used)
    def _():
        wait(slot)
        fetch(i + 1, 1 - slot)
        ff = wd_b.shape[0]
        x = _unpack_halves(xbuf[slot]).astype(BF16)
        gu = _dot(x, wgu_b[...])
        h = (_silu(gu[:, :ff]) * gu[:, ff:]).astype(BF16)
        y_ref[...] = _pack_halves(_dot(h, wd_b[...]))


def _experts(block_expert, n_used, src_groups, xs, wgu, wd, layer, n_blocks):
    w = xs.shape[1]
    d, ff2 = wgu.shape[2:]
    rows = EXPERT_ROWS
    grid_spec = pltpu.PrefetchScalarGridSpec(
        num_scalar_prefetch=3,
        grid=(n_blocks,),
        in_specs=[pl.BlockSpec(memory_space=pl.ANY),
                  pl.BlockSpec((None, None, d, ff2), lambda i, be, nu, sg: (layer, be[i], 0, 0)),
                  pl.BlockSpec((None, None, ff2 // 2, d), lambda i, be, nu, sg: (layer, be[i], 0, 0))],
        out_specs=pl.BlockSpec((rows, w), lambda i, be, nu, sg: (i, 0)),
        scratch_shapes=[pltpu.VMEM((2, rows, w), U32), pltpu.SemaphoreType.DMA((2,)),
                        pltpu.VMEM((d, ff2), BF16), pltpu.VMEM((ff2 // 2, d), BF16)],
    )
    return pl.pallas_call(
        _experts_kernel,
        grid_spec=grid_spec,
        out_shape=jax.ShapeDtypeStruct((n_blocks * rows, w), U32),
        compiler_params=_cparams("arbitrary"),
        name="moe_experts",
    )(block_expert, n_used, src_groups, xs, wgu, wd)


def _combine_kernel(wide_ref, nwide_ref, single_ref, nsingle_ref, x_ref, gates_ref, y_ref, o_ref, ybuf, sems):
    i = pl.program_id(0)
    n_tiles = pl.num_programs(0)
    tm = x_ref.shape[0]
    lr = ybuf.shape[1]
    max_wide = wide_ref.shape[0] // nwide_ref.shape[0]
    max_single = single_ref.shape[0] // nsingle_ref.shape[0]
    slot = i % 2

    def fetch(tile, s):
        def copies(tab_ref, base, n, groups, priority):
            def start(p, carry):
                code = tab_ref[base + p]
                src = pl.multiple_of(lax.shift_right_logical(code, COPY_DST_BITS) * RUN_ROWS, RUN_ROWS)
                dst = pl.multiple_of((code & ((1 << COPY_DST_BITS) - 1)) * RUN_ROWS, RUN_ROWS)
                pltpu.make_async_copy(y_ref.at[pl.ds(src, groups * RUN_ROWS)],
                                      ybuf.at[s, pl.ds(dst, groups * RUN_ROWS)], sems.at[s]).start(priority=priority)
                return carry
            lax.fori_loop(0, n, start, 0)
        copies(wide_ref, tile * max_wide, nwide_ref[tile], COPY_GROUPS, 0)
        copies(single_ref, tile * max_single, nsingle_ref[tile], 1, 1)

    def wait(s):
        pltpu.make_async_copy(y_ref.at[pl.ds(0, lr)], ybuf.at[s], sems.at[s]).wait()

    @pl.when(i == 0)
    def _():
        fetch(0, 0)

    wait(slot)
    fetch(jnp.minimum(i + 1, n_tiles - 1), 1 - slot)
    ys = _unpack_halves(ybuf[slot]).astype(BF16)
    table = gates_ref[...]
    col = lax.broadcasted_iota(I32, (tm, lr), 1)
    pick = jnp.zeros((tm, lr), F32)
    for k in range(MOE_TOP_K):
        jk = table[:, MOE_TOP_K + k:MOE_TOP_K + k + 1].astype(I32)
        pick = jnp.where(col == jk, table[:, k:k + 1], pick)
    o_ref[...] = x_ref[...] + _dot(pick.astype(BF16), ys)

    @pl.when(i == n_tiles - 1)
    def _():
        wait(1 - slot)


def _combine(copy_lists, xnew, gates, y, lr):
    t, d = xnew.shape
    tm = ROW_TILE
    grid_spec = pltpu.PrefetchScalarGridSpec(
        num_scalar_prefetch=len(copy_lists),
        grid=(t // tm,),
        in_specs=[pl.BlockSpec((tm, d), lambda i, *p: (i, 0)),
                  pl.BlockSpec((tm, LANES), lambda i, *p: (i, 0)),
                  pl.BlockSpec(memory_space=pl.ANY)],
        out_specs=pl.BlockSpec((tm, d), lambda i, *p: (i, 0)),
        scratch_shapes=[pltpu.VMEM((2, lr, y.shape[1]), U32), pltpu.SemaphoreType.DMA((2,))],
    )
    return pl.pallas_call(
        _combine_kernel,
        grid_spec=grid_spec,
        out_shape=jax.ShapeDtypeStruct((t, d), F32),
        compiler_params=_cparams("arbitrary"),
        name="moe_combine",
    )(*copy_lists, xnew, gates, y)


def _moe_plan(counts, n_groups, n_experts, lr):
    n_tiles = counts.shape[0]
    gpb = EXPERT_ROWS // RUN_ROWS
    gpt = lr // RUN_ROWS
    cnt = counts[:, 0, n_groups:n_groups + n_experts].astype(I32)
    run = (cnt + RUN_ROWS - 1) // RUN_ROWS
    run_end = jnp.cumsum(run, axis=1)
    run_off = run_end - run
    used = run_end[:, -1]
    seg = jnp.sum(run, axis=0)
    seg_pad = (seg + gpb - 1) // gpb * gpb
    seg_end = jnp.cumsum(seg_pad)
    seg_start = seg_end - seg_pad
    before = jnp.cumsum(run, axis=0) - run
    n_blocks = (n_tiles * gpt + n_experts * (gpb - 1)) // gpb + 1
    n_used = (seg_end[-1] // gpb).astype(I32)
    blk = jnp.arange(n_blocks, dtype=I32)
    first = jnp.minimum(blk, n_used - 1) * gpb
    block_expert = jnp.minimum(jnp.sum(seg_end[None, :] <= first[:, None], axis=1), n_experts - 1).astype(I32)

    assert gpt < (1 << COPY_DST_BITS) and gpt - MOE_TOP_K * ROW_TILE // RUN_ROWS <= gpb
    length = jnp.concatenate([run, (gpt - used)[:, None]], axis=1)
    local = jnp.concatenate([run_off, used[:, None]], axis=1)
    glob = jnp.concatenate([seg_start[None, :] + before, jnp.full((n_tiles, 1), (n_blocks - 1) * gpb, I32)], axis=1)
    n_runs = length.shape[1]
    wide = length // COPY_GROUPS

    def copy_list(per_run, first, n_slots, step):
        end = jnp.cumsum(per_run, axis=1)
        p = jnp.arange(n_slots, dtype=I32)
        owner = jnp.minimum(jnp.sum(end[:, None, :] <= p[None, :, None], axis=2), n_runs - 1)
        own = (owner[:, :, None] == jnp.arange(n_runs, dtype=I32)[None, None, :]).astype(I32)
        sel = lambda tab: jnp.sum(own * tab[:, None, :], axis=2)
        off = sel(first) + (p[None, :] - sel(end - per_run)) * step
        code = ((sel(glob) + off) << COPY_DST_BITS) + sel(local) + off
        return code.astype(I32).reshape(-1), end[:, -1].astype(I32)

    wide_list = copy_list(wide, jnp.zeros_like(wide), gpt // COPY_GROUPS, COPY_GROUPS)
    single_list = copy_list(length - wide * COPY_GROUPS, wide * COPY_GROUPS, n_runs * (COPY_GROUPS - 1), 1)
    copy_lists = (*wide_list, *single_list)

    pick = lambda onehot, tab: jnp.dot(onehot, tab.astype(F32), precision=lax.Precision.HIGHEST)
    zero_src = gpt - 1
    g = jnp.arange(n_blocks * gpb, dtype=I32)
    ge = jnp.minimum(jnp.sum(seg_end[None, :] <= g[:, None], axis=1), n_experts - 1)
    oh_e = (ge[:, None] == jnp.arange(n_experts, dtype=I32)[None, :]).astype(F32)
    per_e = pick(oh_e, jnp.stack([seg_start, seg], axis=1))
    m = g.astype(F32) - per_e[:, 0]
    tile_end = pick(oh_e, jnp.cumsum(run, axis=0).T)
    tile = jnp.minimum(jnp.sum(tile_end <= m[:, None], axis=1), n_tiles - 1)
    oh_t = (tile[:, None] == jnp.arange(n_tiles, dtype=I32)[None, :]).astype(F32)
    tile_base = (jnp.arange(n_tiles, dtype=I32)[:, None] * gpt + run_off - before).T
    src = jnp.sum(oh_t * pick(oh_e, tile_base), axis=1) + m
    src = jnp.where(m < per_e[:, 1], src, float(zero_src)).astype(I32)
    return block_expert, n_used.reshape(1), src, copy_lists, n_blocks


def _moe(xnew, xs, gates, counts, wgu, wd, layer, n_groups):
    n_experts = wgu.shape[1]
    lr = _sorted_rows(ROW_TILE, n_experts)
    block_expert, n_used, src, copy_lists, n_blocks = _moe_plan(counts, n_groups, n_experts, lr)
    y = _experts(block_expert, n_used, src, xs, wgu, wd, layer, n_blocks)
    return _combine(copy_lists, xnew, gates, y, lr)


def _pw1_kernel(x_ref, g_ref, w_ref, b_ref, u_ref):
    ch = u_ref.shape[1]
    xn = _rms(x_ref[...], g_ref[...]).astype(BF16)
    y = _dot(xn, w_ref[...]) + b_ref[...]
    u_ref[...] = (y[:, :ch] * jax.nn.sigmoid(y[:, ch:])).astype(BF16)


def _pw1(x2, g, w, b):
    t, d = x2.shape
    tm = ROW_TILE
    ch = w.shape[1] // 2
    full = lambda a: pl.BlockSpec(a.shape, lambda i: (0,) * a.ndim)
    return pl.pallas_call(
        _pw1_kernel,
        grid=(t // tm,),
        in_specs=[pl.BlockSpec((tm, d), lambda i: (i, 0)), full(g), full(w), full(b)],
        out_specs=pl.BlockSpec((tm, ch), lambda i: (i, 0)),
        out_shape=jax.ShapeDtypeStruct((t, ch), BF16),
        compiler_params=_cparams("parallel"),
        name="pw1_glu",
    )(x2, g, w, b)


def _dwconv_kernel(cur_ref, halo_ref, w_ref, b_ref, lg_ref, lb_ref, o_ref, xs_ref, acc_ref):
    ts, ch = cur_ref.shape
    n_strip, n_tap = w_ref.shape[:2]
    n_chunk = ts // CONV_ROWS
    first_tap = CONV_HALO - (n_tap - 1)
    halo = halo_ref[...].astype(F32)
    halo = jnp.where(pl.program_id(1) > 0, halo, jnp.zeros_like(halo))
    cur = cur_ref[...].astype(F32)
    shifted_rows = ts + CONV_HALO - SUBLANES
    for c in range(n_strip):
        cs = slice(c * LANES, (c + 1) * LANES)
        xs_ref[0, c, :CONV_HALO, :] = halo[:, cs]
        xs_ref[0, c, CONV_HALO:, :] = cur[:, cs]
        for s in range(1, SUBLANES):
            xs_ref[s, c, :shifted_rows, :] = xs_ref[0, c, s:s + shifted_rows, :]

    def strip_chunk(idx, carry):
        r = idx // n_strip
        c = idx % n_strip
        row0 = pl.multiple_of(r * CONV_ROWS, CONV_ROWS)
        acc = jnp.broadcast_to(b_ref[c], (CONV_ROWS, LANES))
        for s in range(SUBLANES):
            taps = [k for k in range(n_tap) if (first_tap + k) % SUBLANES == s]
            lo = (first_tap + taps[0]) // SUBLANES * SUBLANES
            hi = (first_tap + taps[-1]) // SUBLANES * SUBLANES
            data = xs_ref[s, c, pl.ds(row0 + lo, hi - lo + CONV_ROWS), :]
            for k in taps:
                a = (first_tap + k) // SUBLANES * SUBLANES - lo
                acc = acc + w_ref[c, k:k + 1, :] * data[a:a + CONV_ROWS, :]
        acc_ref[c, pl.ds(row0, CONV_ROWS), :] = acc
        return carry

    lax.fori_loop(0, n_chunk * n_strip, strip_chunk, 0, unroll=2)
    y = jnp.concatenate([acc_ref[c] for c in range(n_strip)], axis=1)
    mu = jnp.mean(y, axis=-1, keepdims=True)
    cen = y - mu
    var = jnp.mean(cen * cen, axis=-1, keepdims=True)
    v = cen * lax.rsqrt(var + LN_EPS) * lg_ref[...] + lb_ref[...]
    o_ref[...] = _silu(v).astype(BF16)


def _dwconv(u, w, b, lg, lb, bsz, seq):
    ts = CONV_TILE
    nt = seq // ts
    ch = u.shape[1]
    hpt = ts // CONV_HALO
    n_strip = ch // LANES
    full = lambda a: pl.BlockSpec(a.shape, lambda bb, i: (0,) * a.ndim)
    w = w.reshape(-1, n_strip, LANES).transpose(1, 0, 2)
    b = b.reshape(n_strip, 1, LANES)
    return pl.pallas_call(
        _dwconv_kernel,
        grid=(bsz, nt),
        in_specs=[pl.BlockSpec((ts, ch), lambda bb, i: (bb * nt + i, 0)),
                  pl.BlockSpec((CONV_HALO, ch), lambda bb, i: (jnp.maximum((bb * nt + i) * hpt - 1, 0), 0)),
                  full(w), full(b), full(lg), full(lb)],
        out_specs=pl.BlockSpec((ts, ch), lambda bb, i: (bb * nt + i, 0)),
        out_shape=jax.ShapeDtypeStruct(u.shape, BF16),
        scratch_shapes=[pltpu.VMEM((SUBLANES, n_strip, CONV_HALO + ts, LANES), F32),
                        pltpu.VMEM((n_strip, ts, LANES), F32)],
        compiler_params=_cparams("parallel", "parallel"),
        name="dwconv_ln",
    )(u, u, w, b, lg, lb)


def _row(v):
    return v.reshape(1, -1).astype(F32)


def _pad_lanes(v, n=LANES):
    v = v.reshape(1, -1).astype(F32)
    return jnp.pad(v, ((0, 0), (0, n - v.shape[1])))


def _router_params(w_group, b_group, w_expert, b_expert):
    wr = jnp.concatenate([w_group, w_expert], axis=1)
    wr = jnp.pad(wr, ((0, 0), (0, LANES - wr.shape[1]))).astype(BF16)
    br = _pad_lanes(jnp.concatenate([b_group.reshape(-1), b_expert.reshape(-1)]))
    return wr, br


def kernel(x, mix_norm_g, w_in, q_norm_g, k_norm_g, attn_sinks, ssm_conv_w, ssm_conv_b, ssm_dt_bias, ssm_a_log, ssm_d, ssm_out_norm_g, w_out, conv_norm_g, conv_w_pw1, conv_b_pw1, conv_w_dw, conv_b_dw, conv_ln_g, conv_ln_b, conv_w_pw2, conv_b_pw2, moe_norm_g, moe_w_group, moe_b_group, moe_w_expert, moe_b_expert, moe_w_gate_up, moe_w_down):
    bsz, seq, d = x.shape
    t = bsz * seq
    n_heads = attn_sinks.shape[1]
    n_kv = n_heads // GQA_REP
    q_cols = n_heads * HEAD_DIM
    kv_cols = n_kv * HEAD_DIM
    d_inner = ssm_out_norm_g.shape[1]
    ssm_heads = ssm_a_log.shape[1]
    n_groups = moe_w_group.shape[2]
    n_experts = moe_w_expert.shape[2]
    assert t % ROW_TILE == 0 and seq % CONV_TILE == 0 and seq % (SSM_CHUNK * SSD_STEP_CHUNKS) == 0 and seq % (ATTN_BLOCK * ATTN_STEP_BLOCKS) == 0
    assert ssm_heads <= LANES and n_groups + n_experts <= LANES

    x2 = x.reshape(t, d)
    tri_strict = jnp.asarray(np.triu(np.ones((ROW_TILE, ROW_TILE), np.float32), 1), BF16)
    lane_before = jnp.asarray(np.tril(np.ones((LANES, LANES), np.float32), -1), BF16)
    zero_bias = jnp.zeros((1, d), F32)

    w = w_in[0]
    cuts = [0] + np.cumsum([q_cols, kv_cols, kv_cols, d_inner, ssm_conv_w.shape[2]]).tolist() + [w.shape[1]]
    wq, wk, wv, wz, wx, wdt = (w[:, lo:hi] for lo, hi in zip(cuts[:-1], cuts[1:]))
    wdt = jnp.pad(wdt, ((0, 0), (0, LANES - ssm_heads)))
    seg_ones = jnp.asarray(np.kron(np.eye(q_cols // HEAD_DIM, dtype=np.float32),
                                   np.ones((HEAD_DIM, HEAD_DIM), np.float32)), BF16)
    qg = jnp.tile(_row(q_norm_g[0]), (1, n_heads))
    kg = jnp.tile(_row(k_norm_g[0]), (1, n_kv))
    q, k, v, z, xbc, dt = _inproj(
        x2, _row(mix_norm_g[0]), wq.astype(BF16), wk.astype(BF16), wv.astype(BF16), wz.astype(BF16),
        wx.astype(BF16), wdt.astype(BF16), qg, kg, seg_ones)
    kv_rep = jnp.asarray(np.stack([np.kron(np.eye(n_kv, dtype=np.float32)[:, g:g + 1],
                                           np.tile(np.eye(HEAD_DIM, dtype=np.float32), (1, GQA_REP)))
                                   for g in range(n_kv)]), BF16)
    y_attn = _attention(attn_sinks[0].reshape(1, n_heads).astype(F32), kv_rep, q, k, v, bsz, seq)
    expand = jnp.asarray(np.kron(np.eye(LANES, ssm_heads, dtype=np.float32),
                                 np.ones((1, SSM_HEAD_DIM), np.float32)), BF16)
    tri_incl = jnp.asarray(np.tril(np.ones((SSM_CHUNK, SSM_CHUNK), np.float32)), BF16)
    n_tap = ssm_conv_w.shape[1]
    shift = jnp.asarray(np.concatenate([np.eye(SSM_CHUNK, 2 * SSM_CHUNK, SSM_CHUNK - j, dtype=np.float32)
                                        for j in range(1, n_tap)], axis=0), BF16)
    y_ssm = _ssd(xbc, z, dt, ssm_conv_w[0].astype(F32), _row(ssm_conv_b[0]), _pad_lanes(ssm_dt_bias[0]),
                 _pad_lanes(ssm_a_log[0]), jnp.repeat(_row(ssm_d[0]), SSM_HEAD_DIM, axis=1), _row(ssm_out_norm_g[0]),
                 expand, tri_incl, shift, bsz, seq)
    wo = w_out[0].astype(BF16)
    wr, br = _router_params(moe_w_group[0], moe_b_group[0], moe_w_expert[0], moe_b_expert[0])
    xnew, xs, gates, counts = _proj_router(
        x2, [y_attn, y_ssm], [wo[:q_cols], wo[q_cols:]], zero_bias, _row(moe_norm_g[0]), wr, br, tri_strict,
        lane_before, n_groups, n_experts)
    x2 = _moe(xnew, xs, gates, counts, moe_w_gate_up, moe_w_down, 0, n_groups)

    u = _pw1(x2, _row(conv_norm_g[0]), conv_w_pw1[0].astype(BF16), _row(conv_b_pw1[0]))
    u = _dwconv(u, conv_w_dw[0].astype(F32), _row(conv_b_dw[0]), _row(conv_ln_g[0]), _row(conv_ln_b[0]), bsz, seq)
    wr, br = _router_params(moe_w_group[1], moe_b_group[1], moe_w_expert[1], moe_b_expert[1])
    xnew, xs, gates, counts = _proj_router(
        x2, [u], [conv_w_pw2[0].astype(BF16)], _row(conv_b_pw2[0]), _row(moe_norm_g[1]), wr, br, tri_strict,
        lane_before, n_groups, n_experts)
    x2 = _moe(xnew, xs, gates, counts, moe_w_gate_up, moe_w_down, 1, n_groups)
    return x2.reshape(bsz, seq, d)
```

```python
import functools

import numpy as np
import jax
import jax.numpy as jnp
from jax import lax
from jax.experimental import pallas as pl
from jax.experimental.pallas import tpu as pltpu

F32 = jnp.float32
BF16 = jnp.bfloat16
I32 = jnp.int32
U32 = jnp.uint32

HEAD_DIM = 64
GQA_REP = 4
ATTN_BLOCK = 128
SSM_HEAD_DIM = 64
SSM_GROUPS = 2
SSM_D_STATE = 128
SSM_CHUNK = 128
MOE_TOP_K = 2
RMS_EPS = 1e-6
LN_EPS = 1e-5

LANES = 128
SUBLANES = 8

ROW_TILE = 512
PROJ_TILE = 1024
EXPERT_ROWS = 512
SSD_STEP_CHUNKS = 8
ATTN_STEP_BLOCKS = 8
CONV_TILE = 512
CONV_HALO = 32
CONV_ROWS = 64
RUN_ROWS = SUBLANES
COPY_GROUPS = 4
COPY_DST_BITS = 8
VMEM_LIMIT = 48 * 1024 * 1024


def _cparams(*sem):
    return pltpu.CompilerParams(dimension_semantics=sem, vmem_limit_bytes=VMEM_LIMIT)


def _dot(a, b):
    return jnp.dot(a, b, preferred_element_type=F32)


def _rms(x, g):
    return x * lax.rsqrt(jnp.mean(x * x, axis=-1, keepdims=True) + RMS_EPS) * g


def _silu(x):
    return x * jax.nn.sigmoid(x)


def _pack_halves(y, is_bf16=False):
    c = y.shape[1] // 2
    if is_bf16:
        return lax.bitcast_convert_type(y[:, :c], U32) | (lax.bitcast_convert_type(y[:, c:], U32) >> 16)
    hi = lax.bitcast_convert_type(y[:, :c].astype(BF16).astype(F32), U32)
    lo = lax.bitcast_convert_type(y[:, c:].astype(BF16).astype(F32), U32)
    return (hi & jnp.uint32(0xFFFF0000)) | (lo >> 16)


def _unpack_halves(u):
    hi = lax.bitcast_convert_type(u & jnp.uint32(0xFFFF0000), F32)
    lo = lax.bitcast_convert_type(u << 16, F32)
    return jnp.concatenate([hi, lo], axis=1)


def _inproj_kernel(x_ref, g_ref, wq_ref, wk_ref, wv_ref, wz_ref, wx_ref, wdt_ref, qg_ref, kg_ref, bd_ref,
                   q_out, k_out, v_out, z_out, xbc_out, dt_out):
    xn = _rms(x_ref[...], g_ref[...]).astype(BF16)

    def head_rms(y, gain):
        bd = bd_ref[:y.shape[1], :y.shape[1]]
        sq = y * y
        hi = sq.astype(BF16)
        lo = (sq - hi.astype(F32)).astype(BF16)
        ss = _dot(hi, bd) + _dot(lo, bd)
        return y * lax.rsqrt(ss * (1.0 / HEAD_DIM) + RMS_EPS) * gain

    q = head_rms(_dot(xn, wq_ref[...]), qg_ref[...])
    q_out[...] = (q * (HEAD_DIM ** -0.5)).astype(BF16)
    k_out[...] = head_rms(_dot(xn, wk_ref[...]), kg_ref[...]).astype(BF16)
    v_out[...] = _dot(xn, wv_ref[...]).astype(BF16)
    z_out[...] = _dot(xn, wz_ref[...]).astype(BF16)
    xbc_out[...] = _dot(xn, wx_ref[...]).astype(BF16)
    dt_out[...] = _dot(xn, wdt_ref[...])


def _inproj(x2, g, wq, wk, wv, wz, wx, wdt, qg, kg, bd):
    t, d = x2.shape
    tm = PROJ_TILE
    row = lambda n: pl.BlockSpec((tm, n), lambda i: (i, 0))
    full = lambda a: pl.BlockSpec(a.shape, lambda i: (0,) * a.ndim)
    outs = [(wq.shape[1], BF16), (wk.shape[1], BF16), (wv.shape[1], BF16), (wz.shape[1], BF16),
            (wx.shape[1], BF16), (wdt.shape[1], F32)]
    return pl.pallas_call(
        _inproj_kernel,
        grid=(t // tm,),
        in_specs=[row(d)] + [full(a) for a in (g, wq, wk, wv, wz, wx, wdt, qg, kg, bd)],
        out_specs=[row(n) for n, _ in outs],
        out_shape=[jax.ShapeDtypeStruct((t, n), dt) for n, dt in outs],
        compiler_params=_cparams("parallel"),
        name="inproj",
    )(x2, g, wq, wk, wv, wz, wx, wdt, qg, kg, bd)


def _attn_kernel(sink_ref, rep_ref, q_ref, kc_ref, kp_ref, vc_ref, vp_ref, o_ref):
    n = pl.program_id(1)
    blk = ATTN_BLOCK
    gw = GQA_REP * HEAD_DIM
    n_kv = q_ref.shape[1] // gw
    qi = lax.broadcasted_iota(I32, (blk, 2 * blk), 0)
    kj = lax.broadcasted_iota(I32, (blk, 2 * blk), 1)
    rel = qi + blk - kj
    in_window = (rel >= 0) & (rel < ATTN_BLOCK)
    first_key = jnp.where(n > 0, 0, blk)
    n_key = (ATTN_STEP_BLOCKS + 1) * blk
    key_head = lax.broadcasted_iota(I32, (n_key, gw), 1) // HEAD_DIM
    out_head = lax.broadcasted_iota(I32, (blk, gw), 1) // HEAD_DIM
    kcat = jnp.concatenate([kp_ref[...], kc_ref[...]], axis=0)
    vcat = jnp.concatenate([vp_ref[...], vc_ref[...]], axis=0)
    for g in range(n_kv):
        kk = _dot(kcat, rep_ref[g]).astype(BF16)
        vv = _dot(vcat, rep_ref[g]).astype(BF16)
        zero = jnp.zeros_like(kk)
        k_heads = [jnp.where(key_head == h, kk, zero) for h in range(GQA_REP)]
        v_heads = [jnp.where(key_head == h, vv, zero) for h in range(GQA_REP)]
        for sb in range(ATTN_STEP_BLOCKS):
            _attn_block(sink_ref, q_ref, o_ref, k_heads, v_heads, sb, g, in_window & (kj >= first_key) if sb == 0
                        else in_window, out_head)


def _attn_block(sink_ref, q_ref, o_ref, k_heads, v_heads, sb, g, band, out_head):
    blk = ATTN_BLOCK
    gw = GQA_REP * HEAD_DIM
    sl = slice(g * gw, (g + 1) * gw)
    rows = slice(sb * blk, (sb + 1) * blk)
    keys = slice(sb * blk, (sb + 2) * blk)
    qg = q_ref[rows, sl]
    kbd = jnp.concatenate([k_heads[h][keys, :] for h in range(GQA_REP)], axis=0)
    vbd = jnp.concatenate([v_heads[h][keys, :] for h in range(GQA_REP)], axis=0)
    s = lax.dot_general(qg, kbd, (((1,), (1,)), ((), ())), preferred_element_type=F32)
    probs, scale = [], None
    for h in range(GQA_REP):
        sh = jnp.where(band, s[:, h * 2 * blk:(h + 1) * 2 * blk], -jnp.inf)
        sink = sink_ref[0, g * GQA_REP + h]
        m = jnp.maximum(jnp.max(sh, axis=-1, keepdims=True), sink)
        p = jnp.exp(sh - m)
        denom = jnp.sum(p, axis=-1, keepdims=True) + jnp.exp(sink - m)
        probs.append(p.astype(BF16))
        inv = jnp.broadcast_to(1.0 / denom, (blk, gw))
        scale = inv if scale is None else jnp.where(out_head == h, inv, scale)
    o = _dot(jnp.concatenate(probs, axis=1), vbd)
    o_ref[rows, sl] = (o * scale).astype(BF16)


def _attention(sinks, kv_rep, q, k, v, bsz, seq):
    blk = ATTN_BLOCK
    spb = ATTN_STEP_BLOCKS
    nb = seq // (blk * spb)
    cur = lambda w: pl.BlockSpec((blk * spb, w), lambda b, n: (b * nb + n, 0))
    prev = lambda w: pl.BlockSpec((blk, w), lambda b, n: (jnp.maximum((b * nb + n) * spb - 1, 0), 0))
    qw, kw = q.shape[1], k.shape[1]
    return pl.pallas_call(
        _attn_kernel,
        grid=(bsz, nb),
        in_specs=[pl.BlockSpec(memory_space=pltpu.SMEM), pl.BlockSpec(kv_rep.shape, lambda b, n: (0, 0, 0)),
                  cur(qw), cur(kw), prev(kw), cur(kw), prev(kw)],
        out_specs=cur(qw),
        out_shape=jax.ShapeDtypeStruct(q.shape, BF16),
        compiler_params=_cparams("parallel", "parallel"),
        name="swa_attention",
    )(sinks, kv_rep, q, k, k, v, v)


def _split3(a):
    a1 = a.astype(BF16)
    r = a - a1.astype(F32)
    a2 = r.astype(BF16)
    return a1, a2, (r - a2.astype(F32)).astype(BF16)


def _ssd_kernel(xbc_ref, z_ref, dt_ref, cw_ref, cb_ref, dtb_ref, alog_ref, dskip_ref, og_ref, expand_ref, tri_ref,
                shift_ref, y_ref, prev_ref, state_ref):
    lc = SSM_CHUNK

    @pl.when(pl.program_id(1) == 0)
    def _():
        prev_ref[...] = jnp.zeros_like(prev_ref)
        state_ref[...] = jnp.zeros_like(state_ref)

    for cc in range(SSD_STEP_CHUNKS):
        rows = slice(cc * lc, (cc + 1) * lc)
        prev = prev_ref[...] if cc == 0 else xbc_ref[(cc - 1) * lc:cc * lc, :]
        _ssd_chunk(xbc_ref[rows, :], prev, z_ref[rows, :], dt_ref[rows, :], cw_ref, cb_ref, dtb_ref, alog_ref,
                   dskip_ref, og_ref, expand_ref, tri_ref, shift_ref, y_ref.at[rows, :], state_ref)
    prev_ref[...] = xbc_ref[(SSD_STEP_CHUNKS - 1) * lc:, :]


def _ssd_chunk(xb, prev, z, dt, cw_ref, cb_ref, dtb_ref, alog_ref, dskip_ref, og_ref, expand_ref, tri_ref, shift_ref,
               y_ref, state_ref):
    lc = xb.shape[0]
    d_inner = z.shape[1]
    n_heads = d_inner // SSM_HEAD_DIM
    gn = SSM_GROUPS * SSM_D_STATE
    hpg = n_heads // SSM_GROUPS
    gw = hpg * SSM_HEAD_DIM
    n_tap = cw_ref.shape[0]

    sh = _dot(shift_ref[...], jnp.concatenate([prev, xb], axis=0))
    acc = cb_ref[...] + cw_ref[n_tap - 1:n_tap, :] * xb.astype(F32)
    for j in range(1, n_tap):
        acc = acc + cw_ref[n_tap - 1 - j:n_tap - j, :] * sh[(j - 1) * lc:j * lc, :]
    u = _silu(acc)
    xs = u[:, :d_inner]
    bm = u[:, d_inner:d_inner + gn].astype(BF16)
    cm = u[:, d_inner + gn:].astype(BF16)

    dtr = dt + dtb_ref[...]
    dt_c = jnp.maximum(dtr, 0.0) + jnp.log1p(jnp.exp(-jnp.abs(dtr)))
    a_c = dt_c * (-jnp.exp(alog_ref[...]))
    cum = _dot(tri_ref[...], jnp.concatenate(_split3(a_c), axis=1))
    acum_c = (cum[:, :LANES] + cum[:, LANES:2 * LANES]) + cum[:, 2 * LANES:]
    acum_r = acum_c.T
    ex = _dot(jnp.concatenate(_split3(dt_c) + _split3(acum_c), axis=0), expand_ref[...])
    dt_e = (ex[:lc] + ex[lc:2 * lc]) + ex[2 * lc:3 * lc]
    acum_e = (ex[3 * lc:4 * lc] + ex[4 * lc:5 * lc]) + ex[5 * lc:]
    xdt = xs * dt_e

    li = lax.broadcasted_iota(I32, (lc, lc), 0)
    si = lax.broadcasted_iota(I32, (lc, lc), 1)
    causal = li >= si
    mats = []
    for g in range(SSM_GROUPS):
        cb = lax.dot_general(cm[:, g * SSM_D_STATE:(g + 1) * SSM_D_STATE], bm[:, g * SSM_D_STATE:(g + 1) * SSM_D_STATE],
                             (((1,), (1,)), ((), ())), preferred_element_type=F32)
        for r in range(hpg):
            h = g * hpg + r
            seg = acum_c[:, h:h + 1] - acum_r[h:h + 1, :]
            decay = jnp.exp(jnp.where(causal, seg, -jnp.inf))
            mats.append((cb * decay).astype(BF16))
    xdt_b = xdt.astype(BF16)
    row_head = lax.broadcasted_iota(I32, (lc, d_inner), 1) // SSM_HEAD_DIM
    zero = jnp.zeros_like(xdt_b)
    xbd = jnp.concatenate([jnp.where(row_head == h, xdt_b, zero) for h in range(n_heads)], axis=0)
    y = _dot(jnp.concatenate(mats, axis=1), xbd)

    a_last = acum_e[lc - 1:lc, :]
    w_state = (xdt * jnp.exp(a_last - acum_e)).astype(BF16)
    y_off = []
    for g in range(SSM_GROUPS):
        ns = slice(g * SSM_D_STATE, (g + 1) * SSM_D_STATE)
        hs = slice(g * gw, (g + 1) * gw)
        h_in = state_ref[:, hs]
        y_off.append(_dot(cm[:, ns], h_in.astype(BF16)))
        new = lax.dot_general(bm[:, ns], w_state[:, hs], (((0,), (0,)), ((), ())), preferred_element_type=F32)
        state_ref[:, hs] = h_in * jnp.exp(a_last[:, hs]) + new
    y = y + jnp.concatenate(y_off, axis=1) * jnp.exp(acum_e)

    y = y + dskip_ref[...] * xs
    y = y * _silu(z.astype(F32))
    y_ref[...] = _rms(y, og_ref[...]).astype(BF16)


def _ssd(xbc, z, dt, cw, cb, dtb, alog, dskip, og, expand, tri, shift, bsz, seq):
    rows = SSM_CHUNK * SSD_STEP_CHUNKS
    ns = seq // rows
    blk = lambda n: pl.BlockSpec((rows, n), lambda b, c: (b * ns + c, 0))
    full = lambda a: pl.BlockSpec(a.shape, lambda b, c: (0,) * a.ndim)
    d_inner = z.shape[1]
    return pl.pallas_call(
        _ssd_kernel,
        grid=(bsz, ns),
        in_specs=[blk(xbc.shape[1]), blk(d_inner), blk(dt.shape[1])]
        + [full(a) for a in (cw, cb, dtb, alog, dskip, og, expand, tri, shift)],
        out_specs=blk(d_inner),
        out_shape=jax.ShapeDtypeStruct(z.shape, BF16),
        scratch_shapes=[pltpu.VMEM((SSM_CHUNK, xbc.shape[1]), BF16), pltpu.VMEM((SSM_D_STATE, d_inner), F32)],
        compiler_params=_cparams("arbitrary", "arbitrary"),
        name="ssd_scan",
    )(xbc, z, dt, cw, cb, dtb, alog, dskip, og, expand, tri, shift)


def _sorted_rows(tm, n_experts):
    return MOE_TOP_K * tm + n_experts * RUN_ROWS


def _proj_router_kernel(n_act, n_groups, n_experts, *refs):
    x_ref = refs[0]
    act_refs = refs[1:1 + n_act]
    w_refs = refs[1 + n_act:1 + 2 * n_act]
    b_ref, g_ref, wr_ref, br_ref, tri_ref, ltri_ref = refs[1 + 2 * n_act:7 + 2 * n_act]
    xnew_ref, xs_ref, gates_ref, counts_ref = refs[7 + 2 * n_act:]
    tm = x_ref.shape[0]
    epg = n_experts // n_groups

    y = x_ref[...] + b_ref[...]
    for a_ref, w_ref in zip(act_refs, w_refs):
        y = y + _dot(a_ref[...], w_ref[...])
    xnew_ref[...] = y
    xn = _rms(y, g_ref[...]).astype(BF16)

    logits = _dot(xn, wr_ref[...]) + br_ref[...]
    rr = -(-(n_groups + n_experts) // SUBLANES) * SUBLANES
    lt = logits.T[:rr, :]
    row = lax.broadcasted_iota(I32, lt.shape, 0)
    big = jnp.int32(LANES)
    neg = -jnp.inf

    def first_argmax(v):
        m = jnp.max(v, axis=0, keepdims=True)
        return m, jnp.min(jnp.where(v == m, row, big), axis=0, keepdims=True)

    gl = jnp.where(row < n_groups, lt, neg)
    gmax, gsel = first_argmax(gl)
    g_w = 1.0 / jnp.sum(jnp.exp(gl - gmax), axis=0, keepdims=True)
    erow = row - n_groups
    in_group = (erow >= 0) & (erow < n_experts) & ((erow // epg) == gsel)
    el = jnp.where(in_group, lt, neg)
    top1, i1 = first_argmax(el)
    top2, i2 = first_argmax(jnp.where(row == i1, neg, el))
    e2 = jnp.exp(top2 - top1)
    w1 = 1.0 / (1.0 + e2)
    gate1 = g_w * w1
    gate2 = g_w * (e2 * w1)

    sel1 = row == i1
    sel2 = row == i2
    onehot = jnp.where(sel1, 1.0, jnp.where(sel2, 1.0, 0.0))
    onehot_b = jnp.concatenate([onehot, jnp.zeros((LANES - rr, tm), F32)], axis=0).astype(BF16)
    before = _dot(onehot_b, tri_ref[...])[:rr, :]
    counts_ref[...] = lax.dot_general(jnp.ones((SUBLANES, tm), BF16), onehot_b, (((1,), (1,)), ((), ())),
                                      preferred_element_type=F32)[0:1, :]
    cnt = jnp.sum(onehot, axis=1, keepdims=True)
    cnt_pad = jnp.floor((cnt + (RUN_ROWS - 1)) * (1.0 / RUN_ROWS)) * RUN_ROWS
    cnt_pad = jnp.concatenate([jnp.broadcast_to(cnt_pad, (rr, LANES)), jnp.zeros((LANES - rr, LANES), F32)], axis=0)
    run_start = _dot(ltri_ref[...], cnt_pad.astype(BF16))[:rr, 0:1]
    slot = run_start + before
    j1 = jnp.sum(jnp.where(sel1, slot, 0.0), axis=0, keepdims=True)
    j2 = jnp.sum(jnp.where(sel2, slot, 0.0), axis=0, keepdims=True)
    trow = lax.broadcasted_iota(I32, (SUBLANES, tm), 0)
    table = jnp.where(trow == 0, gate1, jnp.where(trow == 1, gate2,
                                                  jnp.where(trow == 2, j1, jnp.where(trow == 3, j2, 0.0))))
    gates_ref[...] = jnp.concatenate([table, jnp.zeros((LANES - SUBLANES, tm), F32)], axis=0).T

    j1r = j1.astype(I32)
    j2r = j2.astype(I32)
    ri = lax.broadcasted_iota(I32, (xs_ref.shape[0], tm), 0)
    perm = jnp.where(ri == j1r, 1.0, jnp.where(ri == j2r, 1.0, 0.0)).astype(BF16)
    xs_ref[...] = _pack_halves(_dot(perm, xn), is_bf16=True)


def _proj_router(x2, acts, ws, bias, g, wr, br, tri, ltri, n_groups, n_experts):
    t, d = x2.shape
    tm = ROW_TILE
    n_tiles = t // tm
    lr = _sorted_rows(tm, n_experts)
    row = lambda n: pl.BlockSpec((tm, n), lambda i: (i, 0))
    full = lambda a: pl.BlockSpec(a.shape, lambda i: (0,) * a.ndim)
    kern = functools.partial(_proj_router_kernel, len(acts), n_groups, n_experts)
    return pl.pallas_call(
        kern,
        grid=(n_tiles,),
        in_specs=[row(d)] + [row(a.shape[1]) for a in acts] + [full(a) for a in (*ws, bias, g, wr, br, tri, ltri)],
        out_specs=[row(d), pl.BlockSpec((lr, d // 2), lambda i: (i, 0)), row(LANES),
                   pl.BlockSpec((None, 1, LANES), lambda i: (i, 0, 0))],
        out_shape=[jax.ShapeDtypeStruct((t, d), F32), jax.ShapeDtypeStruct((n_tiles * lr, d // 2), U32),
                   jax.ShapeDtypeStruct((t, LANES), F32), jax.ShapeDtypeStruct((n_tiles, 1, LANES), F32)],
        compiler_params=_cparams("parallel"),
        name="proj_router",
    )(x2, *acts, *ws, bias, g, wr, br, tri, ltri)


def _group_copies(table_ref, base, n, src_ref, dst_ref, sem):
    for q in range(n):
        g = table_ref[base + q]
        pltpu.make_async_copy(src_ref.at[pl.ds(pl.multiple_of(g * RUN_ROWS, RUN_ROWS), RUN_ROWS)],
                              dst_ref.at[pl.ds(q * RUN_ROWS, RUN_ROWS)], sem).start(priority=q % 2)


def _experts_kernel(be_ref, nused_ref, src_ref, xs_ref, wgu_ref, wd_ref, y_ref, xbuf, sems, wgu_b, wd_b):
    i = pl.program_id(0)
    n_used = nused_ref[0]
    rows = y_ref.shape[0]
    gpb = rows // RUN_ROWS
    slot = i % 2

    def fetch(blk, s):
        _group_copies(src_ref, blk * gpb, gpb, xs_ref, xbuf.at[s], sems.at[s])

    def wait(s):
        pltpu.make_async_copy(xs_ref.at[pl.ds(0, rows)], xbuf.at[s], sems.at[s]).wait()

    @pl.when(i == 0)
    def _():
        fetch(0, 0)

    @pl.when((i == 0) | (be_ref[i] != be_ref[jnp.maximum(i - 1, 0)]))
    def _():
        wgu_b[...] = wgu_ref[...].astype(BF16)
        wd_b[...] = wd_ref[...].astype(BF16)

    @pl.when(i >= n_used)
    def _():
        y_ref[...] = jnp.zeros_like(y_ref)

    @pl.when(i == n_used)
    def _():
        wait(slot)

    @pl.when(i < n_used)
    def _():
        wait(slot)
        fetch(i + 1, 1 - slot)
        ff = wd_b.shape[0]
        x = _unpack_halves(xbuf[slot]).astype(BF16)
        gu = _dot(x, wgu_b[...])
        h = (_silu(gu[:, :ff]) * gu[:, ff:]).astype(BF16)
        y_ref[...] = _pack_halves(_dot(h, wd_b[...]))


def _experts(block_expert, n_used, src_groups, xs, wgu, wd, layer, n_blocks):
    w = xs.shape[1]
    d, ff2 = wgu.shape[2:]
    rows = EXPERT_ROWS
    grid_spec = pltpu.PrefetchScalarGridSpec(
        num_scalar_prefetch=3,
        grid=(n_blocks,),
        in_specs=[pl.BlockSpec(memory_space=pl.ANY),
                  pl.BlockSpec((None, None, d, ff2), lambda i, be, nu, sg: (layer, be[i], 0, 0)),
                  pl.BlockSpec((None, None, ff2 // 2, d), lambda i, be, nu, sg: (layer, be[i], 0, 0))],
        out_specs=pl.BlockSpec((rows, w), lambda i, be, nu, sg: (i, 0)),
        scratch_shapes=[pltpu.VMEM((2, rows, w), U32), pltpu.SemaphoreType.DMA((2,)),
                        pltpu.VMEM((d, ff2), BF16), pltpu.VMEM((ff2 // 2, d), BF16)],
    )
    return pl.pallas_call(
        _experts_kernel,
        grid_spec=grid_spec,
        out_shape=jax.ShapeDtypeStruct((n_blocks * rows, w), U32),
        compiler_params=_cparams("arbitrary"),
        name="moe_experts",
    )(block_expert, n_used, src_groups, xs, wgu, wd)


def _combine_kernel(wide_ref, nwide_ref, single_ref, nsingle_ref, x_ref, gates_ref, y_ref, o_ref, ybuf, sems):
    i = pl.program_id(0)
    n_tiles = pl.num_programs(0)
    tm = x_ref.shape[0]
    lr = ybuf.shape[1]
    max_wide = wide_ref.shape[0] // nwide_ref.shape[0]
    max_single = single_ref.shape[0] // nsingle_ref.shape[0]
    slot = i % 2

    def fetch(tile, s):
        def copies(tab_ref, base, n, groups, priority):
            def start(p, carry):
                code = tab_ref[base + p]
                src = pl.multiple_of(lax.shift_right_logical(code, COPY_DST_BITS) * RUN_ROWS, RUN_ROWS)
                dst = pl.multiple_of((code & ((1 << COPY_DST_BITS) - 1)) * RUN_ROWS, RUN_ROWS)
                pltpu.make_async_copy(y_ref.at[pl.ds(src, groups * RUN_ROWS)],
                                      ybuf.at[s, pl.ds(dst, groups * RUN_ROWS)], sems.at[s]).start(priority=priority)
                return carry
            lax.fori_loop(0, n, start, 0)
        copies(wide_ref, tile * max_wide, nwide_ref[tile], COPY_GROUPS, 0)
        copies(single_ref, tile * max_single, nsingle_ref[tile], 1, 1)

    def wait(s):
        pltpu.make_async_copy(y_ref.at[pl.ds(0, lr)], ybuf.at[s], sems.at[s]).wait()

    @pl.when(i == 0)
    def _():
        fetch(0, 0)

    wait(slot)
    fetch(jnp.minimum(i + 1, n_tiles - 1), 1 - slot)
    ys = _unpack_halves(ybuf[slot]).astype(BF16)
    table = gates_ref[...]
    col = lax.broadcasted_iota(I32, (tm, lr), 1)
    pick = jnp.zeros((tm, lr), F32)
    for k in range(MOE_TOP_K):
        jk = table[:, MOE_TOP_K + k:MOE_TOP_K + k + 1].astype(I32)
        pick = jnp.where(col == jk, table[:, k:k + 1], pick)
    o_ref[...] = x_ref[...] + _dot(pick.astype(BF16), ys)

    @pl.when(i == n_tiles - 1)
    def _():
        wait(1 - slot)


def _combine(copy_lists, xnew, gates, y, lr):
    t, d = xnew.shape
    tm = ROW_TILE
    grid_spec = pltpu.PrefetchScalarGridSpec(
        num_scalar_prefetch=len(copy_lists),
        grid=(t // tm,),
        in_specs=[pl.BlockSpec((tm, d), lambda i, *p: (i, 0)),
                  pl.BlockSpec((tm, LANES), lambda i, *p: (i, 0)),
                  pl.BlockSpec(memory_space=pl.ANY)],
        out_specs=pl.BlockSpec((tm, d), lambda i, *p: (i, 0)),
        scratch_shapes=[pltpu.VMEM((2, lr, y.shape[1]), U32), pltpu.SemaphoreType.DMA((2,))],
    )
    return pl.pallas_call(
        _combine_kernel,
        grid_spec=grid_spec,
        out_shape=jax.ShapeDtypeStruct((t, d), F32),
        compiler_params=_cparams("arbitrary"),
        name="moe_combine",
    )(*copy_lists, xnew, gates, y)


def _moe_plan(counts, n_groups, n_experts, lr):
    n_tiles = counts.shape[0]
    gpb = EXPERT_ROWS // RUN_ROWS
    gpt = lr // RUN_ROWS
    cnt = counts[:, 0, n_groups:n_groups + n_experts].astype(I32)
    run = (cnt + RUN_ROWS - 1) // RUN_ROWS
    run_end = jnp.cumsum(run, axis=1)
    run_off = run_end - run
    used = run_end[:, -1]
    seg = jnp.sum(run, axis=0)
    seg_pad = (seg + gpb - 1) // gpb * gpb
    seg_end = jnp.cumsum(seg_pad)
    seg_start = seg_end - seg_pad
    before = jnp.cumsum(run, axis=0) - run
    n_blocks = (n_tiles * gpt + n_experts * (gpb - 1)) // gpb + 1
    n_used = (seg_end[-1] // gpb).astype(I32)
    blk = jnp.arange(n_blocks, dtype=I32)
    first = jnp.minimum(blk, n_used - 1) * gpb
    block_expert = jnp.minimum(jnp.sum(seg_end[None, :] <= first[:, None], axis=1), n_experts - 1).astype(I32)

    assert gpt < (1 << COPY_DST_BITS) and gpt - MOE_TOP_K * ROW_TILE // RUN_ROWS <= gpb
    length = jnp.concatenate([run, (gpt - used)[:, None]], axis=1)
    local = jnp.concatenate([run_off, used[:, None]], axis=1)
    glob = jnp.concatenate([seg_start[None, :] + before, jnp.full((n_tiles, 1), (n_blocks - 1) * gpb, I32)], axis=1)
    n_runs = length.shape[1]
    wide = length // COPY_GROUPS

    def copy_list(per_run, first, n_slots, step):
        end = jnp.cumsum(per_run, axis=1)
        p = jnp.arange(n_slots, dtype=I32)
        owner = jnp.minimum(jnp.sum(end[:, None, :] <= p[None, :, None], axis=2), n_runs - 1)
        own = (owner[:, :, None] == jnp.arange(n_runs, dtype=I32)[None, None, :]).astype(I32)
        sel = lambda tab: jnp.sum(own * tab[:, None, :], axis=2)
        off = sel(first) + (p[None, :] - sel(end - per_run)) * step
        code = ((sel(glob) + off) << COPY_DST_BITS) + sel(local) + off
        return code.astype(I32).reshape(-1), end[:, -1].astype(I32)

    wide_list = copy_list(wide, jnp.zeros_like(wide), gpt // COPY_GROUPS, COPY_GROUPS)
    single_list = copy_list(length - wide * COPY_GROUPS, wide * COPY_GROUPS, n_runs * (COPY_GROUPS - 1), 1)
    copy_lists = (*wide_list, *single_list)

    pick = lambda onehot, tab: jnp.dot(onehot, tab.astype(F32), precision=lax.Precision.HIGHEST)
    zero_src = gpt - 1
    g = jnp.arange(n_blocks * gpb, dtype=I32)
    ge = jnp.minimum(jnp.sum(seg_end[None, :] <= g[:, None], axis=1), n_experts - 1)
    oh_e = (ge[:, None] == jnp.arange(n_experts, dtype=I32)[None, :]).astype(F32)
    per_e = pick(oh_e, jnp.stack([seg_start, seg], axis=1))
    m = g.astype(F32) - per_e[:, 0]
    tile_end = pick(oh_e, jnp.cumsum(run, axis=0).T)
    tile = jnp.minimum(jnp.sum(tile_end <= m[:, None], axis=1), n_tiles - 1)
    oh_t = (tile[:, None] == jnp.arange(n_tiles, dtype=I32)[None, :]).astype(F32)
    tile_base = (jnp.arange(n_tiles, dtype=I32)[:, None] * gpt + run_off - before).T
    src = jnp.sum(oh_t * pick(oh_e, tile_base), axis=1) + m
    src = jnp.where(m < per_e[:, 1], src, float(zero_src)).astype(I32)
    return block_expert, n_used.reshape(1), src, copy_lists, n_blocks


def _moe(xnew, xs, gates, counts, wgu, wd, layer, n_groups):
    n_experts = wgu.shape[1]
    lr = _sorted_rows(ROW_TILE, n_experts)
    block_expert, n_used, src, copy_lists, n_blocks = _moe_plan(counts, n_groups, n_experts, lr)
    y = _experts(block_expert, n_used, src, xs, wgu, wd, layer, n_blocks)
    return _combine(copy_lists, xnew, gates, y, lr)


def _pw1_kernel(x_ref, g_ref, w_ref, b_ref, u_ref):
    ch = u_ref.shape[1]
    xn = _rms(x_ref[...], g_ref[...]).astype(BF16)
    y = _dot(xn, w_ref[...]) + b_ref[...]
    u_ref[...] = (y[:, :ch] * jax.nn.sigmoid(y[:, ch:])).astype(BF16)


def _pw1(x2, g, w, b):
    t, d = x2.shape
    tm = PROJ_TILE
    ch = w.shape[1] // 2
    full = lambda a: pl.BlockSpec(a.shape, lambda i: (0,) * a.ndim)
    return pl.pallas_call(
        _pw1_kernel,
        grid=(t // tm,),
        in_specs=[pl.BlockSpec((tm, d), lambda i: (i, 0)), full(g), full(w), full(b)],
        out_specs=pl.BlockSpec((tm, ch), lambda i: (i, 0)),
        out_shape=jax.ShapeDtypeStruct((t, ch), BF16),
        compiler_params=_cparams("parallel"),
        name="pw1_glu",
    )(x2, g, w, b)


def _dwconv_kernel(cur_ref, halo_ref, w_ref, b_ref, lg_ref, lb_ref, o_ref, xs_ref, acc_ref):
    ts, ch = cur_ref.shape
    n_strip, n_tap = w_ref.shape[:2]
    n_chunk = ts // CONV_ROWS
    first_tap = CONV_HALO - (n_tap - 1)
    halo = halo_ref[...].astype(F32)
    halo = jnp.where(pl.program_id(1) > 0, halo, jnp.zeros_like(halo))
    cur = cur_ref[...].astype(F32)
    shifted_rows = ts + CONV_HALO - SUBLANES
    for c in range(n_strip):
        cs = slice(c * LANES, (c + 1) * LANES)
        xs_ref[0, c, :CONV_HALO, :] = halo[:, cs]
        xs_ref[0, c, CONV_HALO:, :] = cur[:, cs]
        for s in range(1, SUBLANES):
            xs_ref[s, c, :shifted_rows, :] = xs_ref[0, c, s:s + shifted_rows, :]

    def strip_chunk(idx, carry):
        r = idx // n_strip
        c = idx % n_strip
        row0 = pl.multiple_of(r * CONV_ROWS, CONV_ROWS)
        acc = jnp.broadcast_to(b_ref[c], (CONV_ROWS, LANES))
        for s in range(SUBLANES):
            taps = [k for k in range(n_tap) if (first_tap + k) % SUBLANES == s]
            lo = (first_tap + taps[0]) // SUBLANES * SUBLANES
            hi = (first_tap + taps[-1]) // SUBLANES * SUBLANES
            data = xs_ref[s, c, pl.ds(row0 + lo, hi - lo + CONV_ROWS), :]
            for k in taps:
                a = (first_tap + k) // SUBLANES * SUBLANES - lo
                acc = acc + w_ref[c, k:k + 1, :] * data[a:a + CONV_ROWS, :]
        acc_ref[c, pl.ds(row0, CONV_ROWS), :] = acc
        return carry

    lax.fori_loop(0, n_chunk * n_strip, strip_chunk, 0, unroll=2)
    y = jnp.concatenate([acc_ref[c] for c in range(n_strip)], axis=1)
    mu = jnp.mean(y, axis=-1, keepdims=True)
    cen = y - mu
    var = jnp.mean(cen * cen, axis=-1, keepdims=True)
    v = cen * lax.rsqrt(var + LN_EPS) * lg_ref[...] + lb_ref[...]
    o_ref[...] = _silu(v).astype(BF16)


def _dwconv(u, w, b, lg, lb, bsz, seq):
    ts = CONV_TILE
    nt = seq // ts
    ch = u.shape[1]
    hpt = ts // CONV_HALO
    n_strip = ch // LANES
    full = lambda a: pl.BlockSpec(a.shape, lambda bb, i: (0,) * a.ndim)
    w = w.reshape(-1, n_strip, LANES).transpose(1, 0, 2)
    b = b.reshape(n_strip, 1, LANES)
    return pl.pallas_call(
        _dwconv_kernel,
        grid=(bsz, nt),
        in_specs=[pl.BlockSpec((ts, ch), lambda bb, i: (bb * nt + i, 0)),
                  pl.BlockSpec((CONV_HALO, ch), lambda bb, i: (jnp.maximum((bb * nt + i) * hpt - 1, 0), 0)),
                  full(w), full(b), full(lg), full(lb)],
        out_specs=pl.BlockSpec((ts, ch), lambda bb, i: (bb * nt + i, 0)),
        out_shape=jax.ShapeDtypeStruct(u.shape, BF16),
        scratch_shapes=[pltpu.VMEM((SUBLANES, n_strip, CONV_HALO + ts, LANES), F32),
                        pltpu.VMEM((n_strip, ts, LANES), F32)],
        compiler_params=_cparams("parallel", "parallel"),
        name="dwconv_ln",
    )(u, u, w, b, lg, lb)


def _row(v):
    return v.reshape(1, -1).astype(F32)


def _pad_lanes(v, n=LANES):
    v = v.reshape(1, -1).astype(F32)
    return jnp.pad(v, ((0, 0), (0, n - v.shape[1])))


def _router_params(w_group, b_group, w_expert, b_expert):
    wr = jnp.concatenate([w_group, w_expert], axis=1)
    wr = jnp.pad(wr, ((0, 0), (0, LANES - wr.shape[1]))).astype(BF16)
    br = _pad_lanes(jnp.concatenate([b_group.reshape(-1), b_expert.reshape(-1)]))
    return wr, br


def kernel(x, mix_norm_g, w_in, q_norm_g, k_norm_g, attn_sinks, ssm_conv_w, ssm_conv_b, ssm_dt_bias, ssm_a_log, ssm_d, ssm_out_norm_g, w_out, conv_norm_g, conv_w_pw1, conv_b_pw1, conv_w_dw, conv_b_dw, conv_ln_g, conv_ln_b, conv_w_pw2, conv_b_pw2, moe_norm_g, moe_w_group, moe_b_group, moe_w_expert, moe_b_expert, moe_w_gate_up, moe_w_down):
    bsz, seq, d = x.shape
    t = bsz * seq
    n_heads = attn_sinks.shape[1]
    n_kv = n_heads // GQA_REP
    q_cols = n_heads * HEAD_DIM
    kv_cols = n_kv * HEAD_DIM
    d_inner = ssm_out_norm_g.shape[1]
    ssm_heads = ssm_a_log.shape[1]
    n_groups = moe_w_group.shape[2]
    n_experts = moe_w_expert.shape[2]
    assert t % ROW_TILE == 0 and t % PROJ_TILE == 0 and seq % CONV_TILE == 0 and seq % (SSM_CHUNK * SSD_STEP_CHUNKS) == 0 and seq % (ATTN_BLOCK * ATTN_STEP_BLOCKS) == 0
    assert ssm_heads <= LANES and n_groups + n_experts <= LANES

    x2 = x.reshape(t, d)
    tri_strict = jnp.asarray(np.triu(np.ones((ROW_TILE, ROW_TILE), np.float32), 1), BF16)
    lane_before = jnp.asarray(np.tril(np.ones((LANES, LANES), np.float32), -1), BF16)
    zero_bias = jnp.zeros((1, d), F32)

    w = w_in[0]
    cuts = [0] + np.cumsum([q_cols, kv_cols, kv_cols, d_inner, ssm_conv_w.shape[2]]).tolist() + [w.shape[1]]
    wq, wk, wv, wz, wx, wdt = (w[:, lo:hi] for lo, hi in zip(cuts[:-1], cuts[1:]))
    wdt = jnp.pad(wdt, ((0, 0), (0, LANES - ssm_heads)))
    seg_ones = jnp.asarray(np.kron(np.eye(q_cols // HEAD_DIM, dtype=np.float32),
                                   np.ones((HEAD_DIM, HEAD_DIM), np.float32)), BF16)
    qg = jnp.tile(_row(q_norm_g[0]), (1, n_heads))
    kg = jnp.tile(_row(k_norm_g[0]), (1, n_kv))
    q, k, v, z, xbc, dt = _inproj(
        x2, _row(mix_norm_g[0]), wq.astype(BF16), wk.astype(BF16), wv.astype(BF16), wz.astype(BF16),
        wx.astype(BF16), wdt.astype(BF16), qg, kg, seg_ones)
    kv_rep = jnp.asarray(np.stack([np.kron(np.eye(n_kv, dtype=np.float32)[:, g:g + 1],
                                           np.tile(np.eye(HEAD_DIM, dtype=np.float32), (1, GQA_REP)))
                                   for g in range(n_kv)]), BF16)
    y_attn = _attention(attn_sinks[0].reshape(1, n_heads).astype(F32), kv_rep, q, k, v, bsz, seq)
    expand = jnp.asarray(np.kron(np.eye(LANES, ssm_heads, dtype=np.float32),
                                 np.ones((1, SSM_HEAD_DIM), np.float32)), BF16)
    tri_incl = jnp.asarray(np.tril(np.ones((SSM_CHUNK, SSM_CHUNK), np.float32)), BF16)
    n_tap = ssm_conv_w.shape[1]
    shift = jnp.asarray(np.concatenate([np.eye(SSM_CHUNK, 2 * SSM_CHUNK, SSM_CHUNK - j, dtype=np.float32)
                                        for j in range(1, n_tap)], axis=0), BF16)
    y_ssm = _ssd(xbc, z, dt, ssm_conv_w[0].astype(F32), _row(ssm_conv_b[0]), _pad_lanes(ssm_dt_bias[0]),
                 _pad_lanes(ssm_a_log[0]), jnp.repeat(_row(ssm_d[0]), SSM_HEAD_DIM, axis=1), _row(ssm_out_norm_g[0]),
                 expand, tri_incl, shift, bsz, seq)
    wo = w_out[0].astype(BF16)
    wr, br = _router_params(moe_w_group[0], moe_b_group[0], moe_w_expert[0], moe_b_expert[0])
    xnew, xs, gates, counts = _proj_router(
        x2, [y_attn, y_ssm], [wo[:q_cols], wo[q_cols:]], zero_bias, _row(moe_norm_g[0]), wr, br, tri_strict,
        lane_before, n_groups, n_experts)
    x2 = _moe(xnew, xs, gates, counts, moe_w_gate_up, moe_w_down, 0, n_groups)

    u = _pw1(x2, _row(conv_norm_g[0]), conv_w_pw1[0].astype(BF16), _row(conv_b_pw1[0]))
    u = _dwconv(u, conv_w_dw[0].astype(F32), _row(conv_b_dw[0]), _row(conv_ln_g[0]), _row(conv_ln_b[0]), bsz, seq)
    wr, br = _router_params(moe_w_group[1], moe_b_group[1], moe_w_expert[1], moe_b_expert[1])
    xnew, xs, gates, counts = _proj_router(
        x2, [u], [conv_w_pw2[0].astype(BF16)], _row(conv_b_pw2[0]), _row(moe_norm_g[1]), wr, br, tri_strict,
        lane_before, n_groups, n_experts)
    x2 = _moe(xnew, xs, gates, counts, moe_w_gate_up, moe_w_down, 1, n_groups)
    return x2.reshape(bsz, seq, d)
```
